```python
import jax, jax.numpy as jnp
from jax import lax
import numpy as np

D_MODEL = 1024
BATCH = 2
SEQ = 8192
DEPTH = 2

CHUNK = 64
D_MIX = D_MODEL
D_RET = D_MIX // 2
D_SB = D_MIX - D_RET
RET_HEADS = 4
RET_HEAD_DIM = D_RET // RET_HEADS
SB_HEADS = 8
SB_HEAD_DIM = D_SB // SB_HEADS
Q_BLOCK = 128
ROPE_BASE = 10000.0
EPS = 1e-6
ADA_SCALE = 0.2
SPLIT_SIZES = [D_RET] * 4 + [D_SB] * 4
D_IN = sum(SPLIT_SIZES)

kernel_name = "hybrid_retention_stickbreaking_block"


def rms_norm(x, g):
    xf = x.astype(jnp.float32)
    return xf * lax.rsqrt(jnp.mean(xf * xf, axis=-1, keepdims=True) + EPS) * g.astype(jnp.float32)


def split_heads(x, n_heads):
    b, s, _ = x.shape
    return x.reshape(b, s, n_heads, -1).transpose(0, 2, 1, 3)


def merge_heads(x):
    b, h, s, d = x.shape
    return x.transpose(0, 2, 1, 3).reshape(b, s, h * d)


def rotary(x, pos):
    half = x.shape[-1] // 2
    inv = ROPE_BASE ** (-jnp.arange(half, dtype=jnp.float32) / half)
    ang = pos[:, None] * inv[None, :]
    cos, sin = jnp.cos(ang), jnp.sin(ang)
    x1, x2 = x[..., :half], x[..., half:]
    return jnp.concatenate([x1 * cos - x2 * sin, x1 * sin + x2 * cos], axis=-1)


def retention(q, k, v):
    b, h, s, d = q.shape
    n = s // CHUNK
    log_gamma = jnp.log1p(-(2.0 ** (-5.0 - jnp.arange(h, dtype=jnp.float32))))
    idx = jnp.arange(CHUNK, dtype=jnp.float32)
    dmask = jnp.exp(jnp.abs(idx[:, None] - idx[None, :])[None] * log_gamma[:, None, None])
    q_dec = jnp.exp((idx + 1.0)[None, :] * log_gamma[:, None])
    k_dec = jnp.exp((CHUNK - 1.0 - idx)[None, :] * log_gamma[:, None])
    chunk_dec = jnp.exp(CHUNK * log_gamma)

    k = k * (d ** -0.5)
    qc = q.reshape(b, h, n, CHUNK, d)
    kc = k.reshape(b, h, n, CHUNK, d)
    vc = v.reshape(b, h, n, CHUNK, v.shape[-1])

    scores = jnp.einsum('bhncd,bhnmd->bhncm', qc, kc) * dmask[None, :, None]
    o_inner = jnp.einsum('bhncm,bhnme->bhnce', scores, vc)

    kv = jnp.einsum('bhnmd,bhnme->nbhde', kc * k_dec[None, :, None, :, None], vc)

    def step(state, kv_i):
        return state * chunk_dec[None, :, None, None] + kv_i, state

    init = jnp.zeros((b, h, d, v.shape[-1]), dtype=kv.dtype)
    _, prev = lax.scan(step, init, kv)
    o_cross = jnp.einsum('bhncd,nbhde->bhnce', qc, prev) * q_dec[None, :, None, :, None]

    o = (o_inner + o_cross).reshape(b, h, s, v.shape[-1])
    mu = jnp.mean(o, axis=-1, keepdims=True)
    var = jnp.mean(jnp.square(o - mu), axis=-1, keepdims=True)
    return (o - mu) * lax.rsqrt(var + EPS)


def stick_breaking(q, k, v):
    b, h, s, d = q.shape
    nb = s // Q_BLOCK
    scale = d ** -0.5
    q_blocks = q.reshape(b, h, nb, Q_BLOCK, d).transpose(2, 0, 1, 3, 4)
    key_pos = jnp.arange(s)

    def block(args):
        qi, bi = args
        t = bi * Q_BLOCK + jnp.arange(Q_BLOCK)
        mask = key_pos[None, :] < t[:, None]
        z = jnp.einsum('bhqd,bhsd->bhqs', qi, k) * scale
        log_beta = jax.nn.log_sigmoid(z)
        log_keep = jnp.where(mask, jax.nn.log_sigmoid(-z), 0.0)
        suffix = lax.cumsum(log_keep, axis=3, reverse=True) - log_keep
        a = jnp.where(mask, jnp.exp(log_beta + suffix), 0.0)
        return jnp.einsum('bhqs,bhse->bhqe', a, v)

    out = lax.map(block, (q_blocks, jnp.arange(nb)))
    return out.transpose(1, 2, 0, 3, 4).reshape(b, h, s, v.shape[-1])


def hybrid_layer(x, c_act, norm_g, w_ada, b_ada, w_in, w_out, pos):
    mod = c_act @ w_ada + b_ada
    shift, scale, gate = jnp.split(mod, 3, axis=-1)
    h = rms_norm(x, norm_g) * (1.0 + scale[:, None, :]) + shift[:, None, :]
    proj = h @ w_in
    split_at = [int(i) for i in np.cumsum(SPLIT_SIZES)[:-1]]
    rq, rk, rv, rg, sq, sk, sv, sg = jnp.split(proj, split_at, axis=-1)

    y_ret = retention(rotary(split_heads(rq, RET_HEADS), pos),
                      rotary(split_heads(rk, RET_HEADS), pos),
                      split_heads(rv, RET_HEADS))
    y_ret = merge_heads(y_ret) * jax.nn.silu(rg)

    y_sb = stick_breaking(split_heads(sq, SB_HEADS), split_heads(sk, SB_HEADS),
                          split_heads(sv, SB_HEADS))
    y_sb = merge_heads(y_sb) * jax.nn.silu(sg)

    y = jnp.concatenate([y_ret, y_sb], axis=-1) @ w_out
    return x + gate[:, None, :] * y


def setup_inputs(seed: int = 0) -> dict:
    key = jax.random.key(seed)
    ks = jax.random.split(key, 8)
    f32 = jnp.float32
    x = jax.random.normal(ks[0], (BATCH, SEQ, D_MODEL), f32)
    c = jax.random.normal(ks[1], (BATCH, D_MODEL), f32)
    norm_g = 1.0 + 0.02 * jax.random.normal(ks[2], (DEPTH, D_MODEL), f32)
    w_ada = jax.random.normal(ks[3], (DEPTH, D_MODEL, 3 * D_MODEL), f32) * (ADA_SCALE * D_MODEL ** -0.5)
    b_ada = 0.02 * jax.random.normal(ks[4], (DEPTH, 3 * D_MODEL), f32)
    w_in = jax.random.normal(ks[5], (DEPTH, D_MODEL, D_IN), f32) * (D_MODEL ** -0.5)
    w_out = jax.random.normal(ks[6], (DEPTH, D_MIX, D_MODEL), f32) * (D_MIX ** -0.5)
    final_g = 1.0 + 0.02 * jax.random.normal(ks[7], (D_MODEL,), f32)
    return {"x": x, "c": c, "norm_g": norm_g, "w_ada": w_ada, "b_ada": b_ada,
            "w_in": w_in, "w_out": w_out, "final_g": final_g}


def reference(x, c, norm_g, w_ada, b_ada, w_in, w_out, final_g):
    pos = jnp.arange(x.shape[1], dtype=jnp.float32)
    c_act = jax.nn.silu(c.astype(jnp.float32))
    h = x.astype(jnp.float32)
    for layer in range(DEPTH):
        h = hybrid_layer(h, c_act, norm_g[layer], w_ada[layer], b_ada[layer],
                         w_in[layer], w_out[layer], pos)
    return rms_norm(h, final_g).astype(x.dtype)
```

```python
import functools
import math

import jax
import jax.numpy as jnp
import numpy as np
from jax import lax
from jax.experimental import pallas as pl
from jax.experimental.pallas import tpu as pltpu

D_MODEL = 1024
DEPTH = 2
CHUNK = 64
D_RET = 512
D_SB = 512
RET_HEADS = 4
RET_HEAD_DIM = 128
SB_HEADS = 8
SB_HEAD_DIM = 64
ROPE_BASE = 10000.0
EPS = 1e-6

LANES = 128
SB_PAIRS = SB_HEADS * SB_HEAD_DIM // LANES
MIB = 1024 * 1024

ADA_ROWS = 8
ADA_TN = 768
PROJ_TM = 512
PROJ_TN = 512
RET_L = 256
SB_TQ = 256
SB_TK = 256
OUT_TM = 512

F32 = jnp.float32
BF16 = jnp.bfloat16


def _silu(x):
    return x * (1.0 / (1.0 + jnp.exp(-x)))


def _ada_kernel(c_ref, w_ref, b_ref, o_ref):
    c_act = _silu(c_ref[...])
    o_ref[0] = jnp.dot(c_act, w_ref[0], preferred_element_type=F32,
                       precision=lax.Precision.HIGHEST) + b_ref[0]


def _ada(c_pad, w_ada, b_ada):
    n = 3 * D_MODEL
    return pl.pallas_call(
        _ada_kernel,
        grid=(DEPTH, n // ADA_TN),
        in_specs=[
            pl.BlockSpec((ADA_ROWS, D_MODEL), lambda l, j: (0, 0)),
            pl.BlockSpec((1, D_MODEL, ADA_TN), lambda l, j: (l, 0, j)),
            pl.BlockSpec((1, 1, ADA_TN), lambda l, j: (l, 0, j)),
        ],
        out_specs=pl.BlockSpec((1, ADA_ROWS, ADA_TN), lambda l, j: (l, 0, j)),
        out_shape=jax.ShapeDtypeStruct((DEPTH, ADA_ROWS, n), F32),
        name="ada_mod",
    )(c_pad, w_ada, b_ada.reshape(DEPTH, 1, n))


def _inproj_kernel(x_ref, g_ref, scale_ref, shift_ref, w_ref, ret_ref, sb_ref, sg_ref):
    x = x_ref[0]
    inv = lax.rsqrt(jnp.mean(x * x, axis=-1, keepdims=True) + EPS)
    h = (x * inv * g_ref[...]) * (1.0 + scale_ref[0]) + shift_ref[0]
    hb = h.astype(BF16)
    n_ret = 4 * D_RET // PROJ_TN
    n_qkv = 3 * D_SB // PROJ_TN
    for n in range(n_ret):
        cols = slice(n * PROJ_TN, (n + 1) * PROJ_TN)
        ret_ref[0, :, cols] = jnp.dot(hb, w_ref[:, cols], preferred_element_type=F32)
    for n in range(n_qkv):
        wcols = slice((n_ret + n) * PROJ_TN, (n_ret + n + 1) * PROJ_TN)
        p = jnp.dot(hb, w_ref[:, wcols], preferred_element_type=F32)
        if n * PROJ_TN < D_SB:
            p = p * (SB_HEAD_DIM ** -0.5)
        sb_ref[0, :, n * PROJ_TN:(n + 1) * PROJ_TN] = p.astype(BF16)
    wcols = slice((n_ret + n_qkv) * PROJ_TN, (n_ret + n_qkv + 1) * PROJ_TN)
    sg_ref[0] = jnp.dot(hb, w_ref[:, wcols], preferred_element_type=F32)


def _inproj(x, g, scale, shift, w_in_bf16):
    b, s, d = x.shape
    d_in = w_in_bf16.shape[1]
    return pl.pallas_call(
        _inproj_kernel,
        grid=(b, s // PROJ_TM),
        in_specs=[
            pl.BlockSpec((1, PROJ_TM, d), lambda bi, i: (bi, i, 0)),
            pl.BlockSpec((1, d), lambda bi, i: (0, 0)),
            pl.BlockSpec((1, 1, d), lambda bi, i: (bi, 0, 0)),
            pl.BlockSpec((1, 1, d), lambda bi, i: (bi, 0, 0)),
            pl.BlockSpec((d, d_in), lambda bi, i: (0, 0)),
        ],
        out_specs=[
            pl.BlockSpec((1, PROJ_TM, 4 * D_RET), lambda bi, i: (bi, i, 0)),
            pl.BlockSpec((1, PROJ_TM, 3 * D_SB), lambda bi, i: (bi, i, 0)),
            pl.BlockSpec((1, PROJ_TM, D_SB), lambda bi, i: (bi, i, 0)),
        ],
        out_shape=[
            jax.ShapeDtypeStruct((b, s, 4 * D_RET), F32),
            jax.ShapeDtypeStruct((b, s, 3 * D_SB), BF16),
            jax.ShapeDtypeStruct((b, s, D_SB), F32),
        ],
        compiler_params=pltpu.CompilerParams(vmem_limit_bytes=56 * MIB),
        name="in_proj",
    )(x, g.reshape(1, d), scale.reshape(b, 1, d), shift.reshape(b, 1, d), w_in_bf16)


def _retention_tables(seq):
    half = RET_HEAD_DIM // 2
    pos = jnp.arange(seq, dtype=F32)
    inv = ROPE_BASE ** (-jnp.arange(half, dtype=F32) / half)
    ang = pos[:, None] * inv[None, :]
    cos, sin = jnp.cos(ang), jnp.sin(ang)
    cos2 = jnp.concatenate([cos, cos], axis=-1)
    sin2 = jnp.concatenate([-sin, sin], axis=-1)

    log_gamma = jnp.log1p(-(2.0 ** (-5.0 - jnp.arange(RET_HEADS, dtype=F32))))
    idx = jnp.arange(RET_L, dtype=F32)
    diff = idx[:, None] - idx[None, :]
    same = (jnp.arange(RET_L)[:, None] // CHUNK) == (jnp.arange(RET_L)[None, :] // CHUNK)
    expo = jnp.where(same, jnp.abs(diff), diff)
    dmask = jnp.where(same | (diff > 0), jnp.exp(expo[None] * log_gamma[:, None, None]), 0.0)
    q_dec = jnp.exp((idx + 1.0)[None, :] * log_gamma[:, None])
    k_dec = jnp.exp((RET_L - 1.0 - idx)[None, :] * log_gamma[:, None])
    blk_dec = jnp.exp(RET_L * log_gamma)
    rep = lambda t: jnp.broadcast_to(t[:, :, None], (RET_HEADS, RET_L, LANES))
    blk = jnp.broadcast_to(blk_dec[:, None, None], (RET_HEADS, 1, LANES))
    return cos2, sin2, dmask, rep(q_dec), rep(k_dec), blk


def _retention_kernel(q_ref, k_ref, v_ref, g_ref, cos_ref, sin_ref, dm_ref, qd_ref, kd_ref,
                      bd_ref, o_ref, state_ref):
    @pl.when(pl.program_id(1) == 0)
    def _():
        state_ref[...] = jnp.zeros_like(state_ref)

    cos = cos_ref[...]
    sin = sin_ref[...]
    half = RET_HEAD_DIM // 2
    for h in range(RET_HEADS):
        cols = slice(h * RET_HEAD_DIM, (h + 1) * RET_HEAD_DIM)
        q = q_ref[0, :, cols]
        k = k_ref[0, :, cols]
        q = q * cos + pltpu.roll(q, half, axis=1) * sin
        k = (k * cos + pltpu.roll(k, half, axis=1) * sin) * (RET_HEAD_DIM ** -0.5)
        vb = v_ref[0, :, cols].astype(BF16)
        qb = q.astype(BF16)
        scores = lax.dot_general(qb, k.astype(BF16), (((1,), (1,)), ((), ())),
                                 preferred_element_type=F32) * dm_ref[h]
        o = jnp.dot(scores.astype(BF16), vb, preferred_element_type=F32)
        state = state_ref[h]
        o = o + jnp.dot(qb, state.astype(BF16), preferred_element_type=F32) * qd_ref[h]
        kv = lax.dot_general((k * kd_ref[h]).astype(BF16), vb, (((0,), (0,)), ((), ())),
                             preferred_element_type=F32)
        state_ref[h] = state * bd_ref[h] + kv
        mu = jnp.mean(o, axis=-1, keepdims=True)
        oc = o - mu
        var = jnp.mean(oc * oc, axis=-1, keepdims=True)
        y = oc * lax.rsqrt(var + EPS) * _silu(g_ref[0, :, cols])
        o_ref[0, :, cols] = y.astype(o_ref.dtype)


def _retention(ret, tables):
    b, s, _ = ret.shape
    cos2, sin2, dmask, q_dec, k_dec, blk = tables
    col = lambda c: pl.BlockSpec((1, RET_L, D_RET), lambda bi, i: (bi, i, c))
    whole = lambda a: pl.BlockSpec(a.shape, lambda bi, i: (0,) * a.ndim)
    return pl.pallas_call(
        _retention_kernel,
        grid=(b, s // RET_L),
        in_specs=[
            col(0), col(1), col(2), col(3),
            pl.BlockSpec((RET_L, LANES), lambda bi, i: (i, 0)),
            pl.BlockSpec((RET_L, LANES), lambda bi, i: (i, 0)),
            whole(dmask), whole(q_dec), whole(k_dec), whole(blk),
        ],
        out_specs=pl.BlockSpec((1, RET_L, D_RET), lambda bi, i: (bi, i, 0)),
        out_shape=jax.ShapeDtypeStruct((b, s, D_RET), BF16),
        scratch_shapes=[pltpu.VMEM((RET_HEADS, RET_HEAD_DIM, RET_HEAD_DIM), F32)],
        name="retention",
    )(ret, ret, ret, ret, cos2, sin2, dmask, q_dec, k_dec, blk)


def _softplus(z):
    return jnp.maximum(z, 0.0) + jnp.log(1.0 + jnp.exp(-jnp.abs(z)))


def _sb_tile(qm_ref, k_ref, v_ref, tri_ref, acc_ref, carry_ref, row0, key0, mask):
    rows = SB_TQ - row0
    qm = qm_ref[:, row0:, :].reshape(2 * rows, LANES)
    kb = k_ref[0, pl.ds(key0, SB_TK), :]
    vb = v_ref[0, pl.ds(key0, SB_TK), :]
    z = lax.dot_general(qm, kb, (((1,), (1,)), ((), ())), preferred_element_type=F32)
    sp = _softplus(z)
    if mask is not None:
        sp = jnp.where(mask, sp, 0.0)
    hi = sp.astype(BF16)
    lo = (sp - hi.astype(F32)).astype(BF16)
    r = jnp.dot(jnp.concatenate([hi, lo], axis=1), tri_ref[...], preferred_element_type=F32)
    a = jnp.exp(z - r)
    if mask is not None:
        a = jnp.where(mask, a, 0.0)
    pv = jnp.dot(a.astype(BF16), vb, preferred_element_type=F32)
    carry = carry_ref[:, row0:, :].reshape(2 * rows, LANES)
    acc = acc_ref[:, row0:, :].reshape(2 * rows, LANES)
    acc_ref[:, row0:, :] = (acc + jnp.exp(-carry) * pv).reshape(2, rows, LANES)
    total = jnp.sum(sp, axis=1, keepdims=True)
    carry_ref[:, row0:, :] = (carry + total).reshape(2, rows, LANES)


def _sb_kernel(q_ref, k_ref, v_ref, g_ref, tri_ref, o_ref, qm_ref, acc_ref, carry_ref):
    qi = pl.program_id(2)
    lane = lax.broadcasted_iota(jnp.int32, (SB_TQ, LANES), 1)
    first = lane < SB_HEAD_DIM
    q = q_ref[0]
    zero = jnp.zeros_like(q)
    qm_ref[0] = jnp.where(first, q, zero)
    qm_ref[1] = jnp.where(first, zero, q)
    acc_ref[...] = jnp.zeros_like(acc_ref)
    carry_ref[...] = jnp.zeros_like(carry_ref)

    n_diag = SB_TQ // SB_TK
    for d in reversed(range(n_diag)):
        row0 = d * SB_TK
        rows = SB_TQ - row0
        t_loc = lax.broadcasted_iota(jnp.int32, (rows, SB_TK), 0)
        s_loc = lax.broadcasted_iota(jnp.int32, (rows, SB_TK), 1)
        m = s_loc < t_loc
        mask = jnp.concatenate([m, m], axis=0)
        key0 = pl.multiple_of(qi * SB_TQ + row0, SB_TK)
        _sb_tile(qm_ref, k_ref, v_ref, tri_ref, acc_ref, carry_ref, row0, key0, mask)

    n_full = qi * n_diag

    def body(it, c):
        key0 = pl.multiple_of((n_full - 1 - it) * SB_TK, SB_TK)
        _sb_tile(qm_ref, k_ref, v_ref, tri_ref, acc_ref, carry_ref, 0, key0, None)
        return c

    lax.fori_loop(0, n_full, body, 0)

    y = jnp.where(first, acc_ref[0], acc_ref[1]) * _silu(g_ref[0])
    o_ref[0] = y.astype(o_ref.dtype)


def _stick_breaking(sb, sg, tri):
    b, s, _ = sb.shape
    nq = s // SB_TQ
    return pl.pallas_call(
        _sb_kernel,
        grid=(b, SB_PAIRS, nq),
        in_specs=[
            pl.BlockSpec((1, SB_TQ, LANES), lambda bi, p, i: (bi, i, p)),
            pl.BlockSpec((1, s, LANES), lambda bi, p, i: (bi, 0, SB_PAIRS + p)),
            pl.BlockSpec((1, s, LANES), lambda bi, p, i: (bi, 0, 2 * SB_PAIRS + p)),
            pl.BlockSpec((1, SB_TQ, LANES), lambda bi, p, i: (bi, i, p)),
            pl.BlockSpec(tri.shape, lambda bi, p, i: (0, 0)),
        ],
        out_specs=pl.BlockSpec((1, SB_TQ, LANES), lambda bi, p, i: (bi, i, p)),
        out_shape=jax.ShapeDtypeStruct((b, s, D_SB), BF16),
        scratch_shapes=[
            pltpu.VMEM((2, SB_TQ, LANES), BF16),
            pltpu.VMEM((2, SB_TQ, LANES), F32),
            pltpu.VMEM((2, SB_TQ, LANES), F32),
        ],
        compiler_params=pltpu.CompilerParams(vmem_limit_bytes=48 * MIB),
        name="stick_breaking",
    )(sb, sb, sb, sg, tri)


def _outproj_kernel(yr_ref, ys_ref, w_ref, x_ref, gate_ref, fg_ref, o_ref, *, final):
    y = jnp.dot(yr_ref[0], w_ref[:D_RET, :], preferred_element_type=F32)
    y = y + jnp.dot(ys_ref[0], w_ref[D_RET:, :], preferred_element_type=F32)
    xn = x_ref[0] + gate_ref[0] * y
    if final:
        inv = lax.rsqrt(jnp.mean(xn * xn, axis=-1, keepdims=True) + EPS)
        xn = xn * inv * fg_ref[...]
    o_ref[0] = xn.astype(o_ref.dtype)


def _outproj(y_ret, y_sb, w_out_bf16, x, gate, final_g, final):
    b, s, d = x.shape
    return pl.pallas_call(
        functools.partial(_outproj_kernel, final=final),
        grid=(b, s // OUT_TM),
        in_specs=[
            pl.BlockSpec((1, OUT_TM, D_RET), lambda bi, i: (bi, i, 0)),
            pl.BlockSpec((1, OUT_TM, D_SB), lambda bi, i: (bi, i, 0)),
            pl.BlockSpec(w_out_bf16.shape, lambda bi, i: (0, 0)),
            pl.BlockSpec((1, OUT_TM, d), lambda bi, i: (bi, i, 0)),
            pl.BlockSpec((1, 1, d), lambda bi, i: (bi, 0, 0)),
            pl.BlockSpec((1, d), lambda bi, i: (0, 0)),
        ],
        out_specs=pl.BlockSpec((1, OUT_TM, d), lambda bi, i: (bi, i, 0)),
        out_shape=jax.ShapeDtypeStruct((b, s, d), x.dtype),
        name="out_proj_final" if final else "out_proj",
    )(y_ret, y_sb, w_out_bf16, x, gate.reshape(b, 1, d), final_g.reshape(1, d))


def kernel(x, c, norm_g, w_ada, b_ada, w_in, w_out, final_g):
    b, s, d = x.shape
    assert d == D_MODEL and s % max(SB_TQ, RET_L, PROJ_TM, OUT_TM) == 0 and SB_TQ % SB_TK == 0
    c_pad = jnp.zeros((ADA_ROWS, d), F32).at[:b].set(c.astype(F32))
    mod = _ada(c_pad, w_ada, b_ada)[:, :b]
    tables = _retention_tables(s)
    tri1 = jnp.tril(jnp.ones((SB_TK, SB_TK), F32)).astype(BF16)
    tri = jnp.concatenate([tri1, tri1], axis=0)
    w_in_b = w_in.astype(BF16)
    w_out_b = w_out.astype(BF16)

    h = x.astype(F32)
    for layer in range(DEPTH):
        shift, scale, gate = jnp.split(mod[layer], 3, axis=-1)
        ret, sb, sg = _inproj(h, norm_g[layer], scale, shift, w_in_b[layer])
        y_ret = _retention(ret, tables)
        y_sb = _stick_breaking(sb, sg, tri)
        h = _outproj(y_ret, y_sb, w_out_b[layer], h, gate, final_g, layer == DEPTH - 1)
    return h.astype(x.dtype)
```

```python
import functools

import jax
import jax.numpy as jnp
from jax import lax
from jax.experimental import pallas as pl
from jax.experimental.pallas import tpu as pltpu

D_MODEL = 1024
DEPTH = 2
CHUNK = 64
D_RET = 512
D_SB = 512
RET_HEADS = 4
RET_HEAD_DIM = 128
SB_HEADS = 8
SB_HEAD_DIM = 64
ROPE_BASE = 10000.0
EPS = 1e-6
LOG2E = 1.4426950408889634

LANES = 128
SB_PAIRS = SB_HEADS * SB_HEAD_DIM // LANES
MIB = 1024 * 1024

ADA_ROWS = 8
ADA_TN = 768
PROJ_TM = 512
PROJ_TN = 512
RET_L = 256
SB_TQ = 512
SB_TK = 256
OUT_TM = 512

SB_EXP2_CLAMP = 126.0
SB_MASKED = -1e30
SB_Q_SCALE = LOG2E * SB_HEAD_DIM ** -0.5

F32 = jnp.float32
BF16 = jnp.bfloat16


def _silu(x):
    return x * (1.0 / (1.0 + jnp.exp(-x)))


def _ada_kernel(c_ref, w_ref, b_ref, o_ref):
    c_act = _silu(c_ref[...])
    o_ref[0] = jnp.dot(c_act, w_ref[0], preferred_element_type=F32,
                       precision=lax.Precision.HIGHEST) + b_ref[0]


def _ada(c_pad, w_ada, b_ada):
    n = 3 * D_MODEL
    return pl.pallas_call(
        _ada_kernel,
        grid=(DEPTH, n // ADA_TN),
        in_specs=[
            pl.BlockSpec((ADA_ROWS, D_MODEL), lambda l, j: (0, 0)),
            pl.BlockSpec((1, D_MODEL, ADA_TN), lambda l, j: (l, 0, j)),
            pl.BlockSpec((1, 1, ADA_TN), lambda l, j: (l, 0, j)),
        ],
        out_specs=pl.BlockSpec((1, ADA_ROWS, ADA_TN), lambda l, j: (l, 0, j)),
        out_shape=jax.ShapeDtypeStruct((DEPTH, ADA_ROWS, n), F32),
        name="ada_mod",
    )(c_pad, w_ada, b_ada.reshape(DEPTH, 1, n))


def _inproj_kernel(x_ref, g_ref, scale_ref, shift_ref, w_ref, ret_ref, sb_ref, sg_ref):
    x = x_ref[0]
    inv = lax.rsqrt(jnp.mean(x * x, axis=-1, keepdims=True) + EPS)
    h = (x * inv * g_ref[...]) * (1.0 + scale_ref[0]) + shift_ref[0]
    hb = h.astype(BF16)
    n_ret = 4 * D_RET // PROJ_TN
    n_qkv = 3 * D_SB // PROJ_TN
    for n in range(n_ret):
        cols = slice(n * PROJ_TN, (n + 1) * PROJ_TN)
        ret_ref[0, :, cols] = jnp.dot(hb, w_ref[:, cols], preferred_element_type=F32)
    for n in range(n_qkv):
        wcols = slice((n_ret + n) * PROJ_TN, (n_ret + n + 1) * PROJ_TN)
        p = jnp.dot(hb, w_ref[:, wcols], preferred_element_type=F32)
        if n * PROJ_TN < D_SB:
            p = p * SB_Q_SCALE
        sb_ref[0, :, n * PROJ_TN:(n + 1) * PROJ_TN] = p.astype(BF16)
    wcols = slice((n_ret + n_qkv) * PROJ_TN, (n_ret + n_qkv + 1) * PROJ_TN)
    sg_ref[0] = jnp.dot(hb, w_ref[:, wcols], preferred_element_type=F32)


def _inproj(x, g, scale, shift, w_in_bf16):
    b, s, d = x.shape
    d_in = w_in_bf16.shape[1]
    return pl.pallas_call(
        _inproj_kernel,
        grid=(b, s // PROJ_TM),
        in_specs=[
            pl.BlockSpec((1, PROJ_TM, d), lambda bi, i: (bi, i, 0)),
            pl.BlockSpec((1, d), lambda bi, i: (0, 0)),
            pl.BlockSpec((1, 1, d), lambda bi, i: (bi, 0, 0)),
            pl.BlockSpec((1, 1, d), lambda bi, i: (bi, 0, 0)),
            pl.BlockSpec((d, d_in), lambda bi, i: (0, 0)),
        ],
        out_specs=[
            pl.BlockSpec((1, PROJ_TM, 4 * D_RET), lambda bi, i: (bi, i, 0)),
            pl.BlockSpec((1, PROJ_TM, 3 * D_SB), lambda bi, i: (bi, i, 0)),
            pl.BlockSpec((1, PROJ_TM, D_SB), lambda bi, i: (bi, i, 0)),
        ],
        out_shape=[
            jax.ShapeDtypeStruct((b, s, 4 * D_RET), F32),
            jax.ShapeDtypeStruct((b, s, 3 * D_SB), BF16),
            jax.ShapeDtypeStruct((b, s, D_SB), F32),
        ],
        compiler_params=pltpu.CompilerParams(vmem_limit_bytes=56 * MIB),
        name="in_proj",
    )(x, g.reshape(1, d), scale.reshape(b, 1, d), shift.reshape(b, 1, d), w_in_bf16)


def _retention_tables(seq):
    half = RET_HEAD_DIM // 2
    pos = jnp.arange(seq, dtype=F32)
    inv = ROPE_BASE ** (-jnp.arange(half, dtype=F32) / half)
    ang = pos[:, None] * inv[None, :]
    cos, sin = jnp.cos(ang), jnp.sin(ang)
    cos2 = jnp.concatenate([cos, cos], axis=-1)
    sin2 = jnp.concatenate([-sin, sin], axis=-1)

    log_gamma = jnp.log1p(-(2.0 ** (-5.0 - jnp.arange(RET_HEADS, dtype=F32))))
    idx = jnp.arange(RET_L, dtype=F32)
    diff = idx[:, None] - idx[None, :]
    same = (jnp.arange(RET_L)[:, None] // CHUNK) == (jnp.arange(RET_L)[None, :] // CHUNK)
    expo = jnp.where(same, jnp.abs(diff), diff)
    dmask = jnp.where(same | (diff > 0), jnp.exp(expo[None] * log_gamma[:, None, None]), 0.0)
    q_dec = jnp.exp((idx + 1.0)[None, :] * log_gamma[:, None])
    k_dec = jnp.exp((RET_L - 1.0 - idx)[None, :] * log_gamma[:, None])
    blk_dec = jnp.exp(RET_L * log_gamma)
    rep = lambda t: jnp.broadcast_to(t[:, :, None], (RET_HEADS, RET_L, LANES))
    blk = jnp.broadcast_to(blk_dec[:, None, None], (RET_HEADS, 1, LANES))
    return cos2, sin2, dmask, rep(q_dec), rep(k_dec), blk


def _retention_kernel(q_ref, k_ref, v_ref, g_ref, cos_ref, sin_ref, dm_ref, qd_ref, kd_ref,
                      bd_ref, o_ref, state_ref):
    @pl.when(pl.program_id(1) == 0)
    def _():
        state_ref[...] = jnp.zeros_like(state_ref)

    cos = cos_ref[...]
    sin = sin_ref[...]
    half = RET_HEAD_DIM // 2
    for h in range(RET_HEADS):
        cols = slice(h * RET_HEAD_DIM, (h + 1) * RET_HEAD_DIM)
        q = q_ref[0, :, cols]
        k = k_ref[0, :, cols]
        q = q * cos + pltpu.roll(q, half, axis=1) * sin
        k = (k * cos + pltpu.roll(k, half, axis=1) * sin) * (RET_HEAD_DIM ** -0.5)
        vb = v_ref[0, :, cols].astype(BF16)
        qb = q.astype(BF16)
        scores = lax.dot_general(qb, k.astype(BF16), (((1,), (1,)), ((), ())),
                                 preferred_element_type=F32) * dm_ref[h]
        o = jnp.dot(scores.astype(BF16), vb, preferred_element_type=F32)
        state = state_ref[h]
        o = o + jnp.dot(qb, state.astype(BF16), preferred_element_type=F32) * qd_ref[h]
        kv = lax.dot_general((k * kd_ref[h]).astype(BF16), vb, (((0,), (0,)), ((), ())),
                             preferred_element_type=F32)
        state_ref[h] = state * bd_ref[h] + kv
        mu = jnp.mean(o, axis=-1, keepdims=True)
        oc = o - mu
        var = jnp.mean(oc * oc, axis=-1, keepdims=True)
        y = oc * lax.rsqrt(var + EPS) * _silu(g_ref[0, :, cols])
        o_ref[0, :, cols] = y.astype(o_ref.dtype)


def _retention(ret, tables):
    b, s, _ = ret.shape
    cos2, sin2, dmask, q_dec, k_dec, blk = tables
    col = lambda c: pl.BlockSpec((1, RET_L, D_RET), lambda bi, i: (bi, i, c))
    whole = lambda a: pl.BlockSpec(a.shape, lambda bi, i: (0,) * a.ndim)
    return pl.pallas_call(
        _retention_kernel,
        grid=(b, s // RET_L),
        in_specs=[
            col(0), col(1), col(2), col(3),
            pl.BlockSpec((RET_L, LANES), lambda bi, i: (i, 0)),
            pl.BlockSpec((RET_L, LANES), lambda bi, i: (i, 0)),
            whole(dmask), whole(q_dec), whole(k_dec), whole(blk),
        ],
        out_specs=pl.BlockSpec((1, RET_L, D_RET), lambda bi, i: (bi, i, 0)),
        out_shape=jax.ShapeDtypeStruct((b, s, D_RET), BF16),
        scratch_shapes=[pltpu.VMEM((RET_HEADS, RET_HEAD_DIM, RET_HEAD_DIM), F32)],
        name="retention",
    )(ret, ret, ret, ret, cos2, sin2, dmask, q_dec, k_dec, blk)


def _softplus2(u):
    return jnp.maximum(u, jnp.log(1.0 + jnp.exp2(jnp.minimum(u, SB_EXP2_CLAMP))) * LOG2E)


def _sb_scores(qm_ref, k_ref, u_ref, sp_ref, ncarry_ref, slot, row0, key0, diagonal):
    rows = SB_TQ - row0
    qm = qm_ref[:, row0:, :].reshape(2 * rows, LANES)
    kb = k_ref[0, pl.ds(key0, SB_TK), :]
    u = lax.dot_general(qm, kb, (((1,), (1,)), ((), ())), preferred_element_type=F32)
    sp = _softplus2(u)
    ncarry = ncarry_ref[:, row0:, :].reshape(2 * rows, LANES)
    uc = u + jnp.concatenate([ncarry] * (SB_TK // LANES), axis=1)
    if diagonal:
        t_loc = lax.broadcasted_iota(jnp.int32, (rows, SB_TK), 0)
        s_loc = lax.broadcasted_iota(jnp.int32, (rows, SB_TK), 1)
        m = s_loc < t_loc
        mask = jnp.concatenate([m, m], axis=0)
        sp = jnp.where(mask, sp, 0.0)
        uc = jnp.where(mask, uc, SB_MASKED)
    u_ref[slot, :, row0:, :] = uc.reshape(2, rows, SB_TK)
    sp_ref[slot, :, row0:, :] = sp.astype(BF16).reshape(2, rows, SB_TK)
    total = jnp.sum(sp, axis=1, keepdims=True)
    ncarry_ref[:, row0:, :] = (ncarry - total).reshape(2, rows, LANES)


def _sb_values(v_ref, tri_ref, u_ref, sp_ref, acc_ref, slot, row0, key0):
    rows = SB_TQ - row0
    vb = v_ref[0, pl.ds(key0, SB_TK), :]
    sp = sp_ref[slot, :, row0:, :].reshape(2 * rows, SB_TK)
    r = jnp.dot(sp, tri_ref[...], preferred_element_type=F32)
    uc = u_ref[slot, :, row0:, :].reshape(2 * rows, SB_TK)
    a = jnp.exp2(uc - r)
    pv = jnp.dot(a.astype(BF16), vb, preferred_element_type=F32)
    acc = acc_ref[:, row0:, :].reshape(2 * rows, LANES)
    acc_ref[:, row0:, :] = (acc + pv).reshape(2, rows, LANES)


def _sb_kernel(q_ref, k_ref, v_ref, g_ref, tri_ref, o_ref,
               qm_ref, acc_ref, ncarry_ref, u_ref, sp_ref):
    qi = pl.program_id(2)
    lane = lax.broadcasted_iota(jnp.int32, (SB_TQ, LANES), 1)
    first = lane < SB_HEAD_DIM
    q = q_ref[0]
    zero = jnp.zeros_like(q)
    qm_ref[0] = jnp.where(first, q, zero)
    qm_ref[1] = jnp.where(first, zero, q)
    acc_ref[...] = jnp.zeros_like(acc_ref)
    ncarry_ref[...] = jnp.zeros_like(ncarry_ref)

    scores = functools.partial(_sb_scores, qm_ref, k_ref, u_ref, sp_ref, ncarry_ref)
    values = functools.partial(_sb_values, v_ref, tri_ref, u_ref, sp_ref, acc_ref)
    key_of = lambda blk: pl.multiple_of(blk * SB_TK, SB_TK)

    n_full = 2 * qi
    scores(0, SB_TK, key_of(n_full + 1), True)
    scores(1, 0, key_of(n_full), True)
    values(0, SB_TK, key_of(n_full + 1))

    def body(it, c):
        blk = n_full - 1 - 2 * it
        scores(0, 0, key_of(blk), False)
        values(1, 0, key_of(blk + 1))
        scores(1, 0, key_of(blk - 1), False)
        values(0, 0, key_of(blk))
        return c

    lax.fori_loop(0, qi, body, 0)
    values(1, 0, key_of(0))

    y = jnp.where(first, acc_ref[0], acc_ref[1]) * _silu(g_ref[0])
    o_ref[0] = y.astype(o_ref.dtype)


def _stick_breaking(sb, sg, tri):
    b, s, _ = sb.shape
    nq = s // SB_TQ
    return pl.pallas_call(
        _sb_kernel,
        grid=(b, SB_PAIRS, nq),
        in_specs=[
            pl.BlockSpec((1, SB_TQ, LANES), lambda bi, p, i: (bi, i, p)),
            pl.BlockSpec((1, s, LANES), lambda bi, p, i: (bi, 0, SB_PAIRS + p)),
            pl.BlockSpec((1, s, LANES), lambda bi, p, i: (bi, 0, 2 * SB_PAIRS + p)),
            pl.BlockSpec((1, SB_TQ, LANES), lambda bi, p, i: (bi, i, p)),
            pl.BlockSpec(tri.shape, lambda bi, p, i: (0, 0)),
        ],
        out_specs=pl.BlockSpec((1, SB_TQ, LANES), lambda bi, p, i: (bi, i, p)),
        out_shape=jax.ShapeDtypeStruct((b, s, D_SB), BF16),
        scratch_shapes=[
            pltpu.VMEM((2, SB_TQ, LANES), BF16),
            pltpu.VMEM((2, SB_TQ, LANES), F32),
            pltpu.VMEM((2, SB_TQ, LANES), F32),
            pltpu.VMEM((2, 2, SB_TQ, SB_TK), F32),
            pltpu.VMEM((2, 2, SB_TQ, SB_TK), BF16),
        ],
        compiler_params=pltpu.CompilerParams(vmem_limit_bytes=48 * MIB),
        name="stick_breaking",
    )(sb, sb, sb, sg, tri)


def _outproj_kernel(yr_ref, ys_ref, w_ref, x_ref, gate_ref, fg_ref, o_ref, *, final):
    y = jnp.dot(yr_ref[0], w_ref[:D_RET, :], preferred_element_type=F32)
    y = y + jnp.dot(ys_ref[0], w_ref[D_RET:, :], preferred_element_type=F32)
    xn = x_ref[0] + gate_ref[0] * y
    if final:
        inv = lax.rsqrt(jnp.mean(xn * xn, axis=-1, keepdims=True) + EPS)
        xn = xn * inv * fg_ref[...]
    o_ref[0] = xn.astype(o_ref.dtype)


def _outproj(y_ret, y_sb, w_out_bf16, x, gate, final_g, final):
    b, s, d = x.shape
    return pl.pallas_call(
        functools.partial(_outproj_kernel, final=final),
        grid=(b, s // OUT_TM),
        in_specs=[
            pl.BlockSpec((1, OUT_TM, D_RET), lambda bi, i: (bi, i, 0)),
            pl.BlockSpec((1, OUT_TM, D_SB), lambda bi, i: (bi, i, 0)),
            pl.BlockSpec(w_out_bf16.shape, lambda bi, i: (0, 0)),
            pl.BlockSpec((1, OUT_TM, d), lambda bi, i: (bi, i, 0)),
            pl.BlockSpec((1, 1, d), lambda bi, i: (bi, 0, 0)),
            pl.BlockSpec((1, d), lambda bi, i: (0, 0)),
        ],
        out_specs=pl.BlockSpec((1, OUT_TM, d), lambda bi, i: (bi, i, 0)),
        out_shape=jax.ShapeDtypeStruct((b, s, d), x.dtype),
        name="out_proj_final" if final else "out_proj",
    )(y_ret, y_sb, w_out_bf16, x, gate.reshape(b, 1, d), final_g.reshape(1, d))


def kernel(x, c, norm_g, w_ada, b_ada, w_in, w_out, final_g):
    b, s, d = x.shape
    assert d == D_MODEL and s % max(SB_TQ, RET_L, PROJ_TM, OUT_TM) == 0 and SB_TQ == 2 * SB_TK
    c_pad = jnp.zeros((ADA_ROWS, d), F32).at[:b].set(c.astype(F32))
    mod = _ada(c_pad, w_ada, b_ada)[:, :b]
    tables = _retention_tables(s)
    tri = jnp.tril(jnp.ones((SB_TK, SB_TK), F32)).astype(BF16)
    w_in_b = w_in.astype(BF16)
    w_out_b = w_out.astype(BF16)

    h = x.astype(F32)
    for layer in range(DEPTH):
        shift, scale, gate = jnp.split(mod[layer], 3, axis=-1)
        ret, sb, sg = _inproj(h, norm_g[layer], scale, shift, w_in_b[layer])
        y_ret = _retention(ret, tables)
        y_sb = _stick_breaking(sb, sg, tri)
        h = _outproj(y_ret, y_sb, w_out_b[layer], h, gate, final_g, layer == DEPTH - 1)
    return h.astype(x.dtype)
```

```python
import functools

import jax
import jax.numpy as jnp
from jax import lax
from jax.experimental import pallas as pl
from jax.experimental.pallas import tpu as pltpu

D_MODEL = 1024
DEPTH = 2
CHUNK = 64
D_RET = 512
D_SB = 512
RET_HEADS = 4
RET_HEAD_DIM = 128
SB_HEADS = 8
SB_HEAD_DIM = 64
ROPE_BASE = 10000.0
EPS = 1e-6
LOG2E = 1.4426950408889634

LANES = 128
SB_PAIRS = SB_HEADS * SB_HEAD_DIM // LANES
SB_GROUP = 2
MIB = 1024 * 1024

ADA_ROWS = 8
ADA_TN = 768
PROJ_TM = 512
PROJ_TN = 512
RET_L = 256
SB_TQ = 512
SB_TK = 256
OUT_TM = 512

SB_EXP2_CLAMP = 126.0
SB_MASKED = -1e30
SB_Q_SCALE = LOG2E * SB_HEAD_DIM ** -0.5

F32 = jnp.float32
BF16 = jnp.bfloat16


def _silu(x):
    return x * (1.0 / (1.0 + jnp.exp(-x)))


def _ada_kernel(c_ref, w_ref, b_ref, o_ref):
    c_act = _silu(c_ref[...])
    o_ref[0] = jnp.dot(c_act, w_ref[0], preferred_element_type=F32,
                       precision=lax.Precision.HIGHEST) + b_ref[0]


def _ada(c_pad, w_ada, b_ada):
    n = 3 * D_MODEL
    return pl.pallas_call(
        _ada_kernel,
        grid=(DEPTH, n // ADA_TN),
        in_specs=[
            pl.BlockSpec((ADA_ROWS, D_MODEL), lambda l, j: (0, 0)),
            pl.BlockSpec((1, D_MODEL, ADA_TN), lambda l, j: (l, 0, j)),
            pl.BlockSpec((1, 1, ADA_TN), lambda l, j: (l, 0, j)),
        ],
        out_specs=pl.BlockSpec((1, ADA_ROWS, ADA_TN), lambda l, j: (l, 0, j)),
        out_shape=jax.ShapeDtypeStruct((DEPTH, ADA_ROWS, n), F32),
        name="ada_mod",
    )(c_pad, w_ada, b_ada.reshape(DEPTH, 1, n))


def _inproj_kernel(x_ref, g_ref, scale_ref, shift_ref, w_ref, ret_ref, sb_ref, sg_ref):
    x = x_ref[0]
    inv = lax.rsqrt(jnp.mean(x * x, axis=-1, keepdims=True) + EPS)
    h = (x * inv * g_ref[...]) * (1.0 + scale_ref[0]) + shift_ref[0]
    hb = h.astype(BF16)
    n_ret = 4 * D_RET // PROJ_TN
    n_qkv = 3 * D_SB // PROJ_TN
    for n in range(n_ret):
        cols = slice(n * PROJ_TN, (n + 1) * PROJ_TN)
        ret_ref[0, :, cols] = jnp.dot(hb, w_ref[:, cols], preferred_element_type=F32)
    for n in range(n_qkv):
        wcols = slice((n_ret + n) * PROJ_TN, (n_ret + n + 1) * PROJ_TN)
        p = jnp.dot(hb, w_ref[:, wcols], preferred_element_type=F32)
        if n * PROJ_TN < D_SB:
            p = p * SB_Q_SCALE
        sb_ref[0, :, n * PROJ_TN:(n + 1) * PROJ_TN] = p.astype(BF16)
    wcols = slice((n_ret + n_qkv) * PROJ_TN, (n_ret + n_qkv + 1) * PROJ_TN)
    sg_ref[0] = jnp.dot(hb, w_ref[:, wcols], preferred_element_type=F32)


def _inproj(x, g, scale, shift, w_in_bf16):
    b, s, d = x.shape
    d_in = w_in_bf16.shape[1]
    return pl.pallas_call(
        _inproj_kernel,
        grid=(b, s // PROJ_TM),
        in_specs=[
            pl.BlockSpec((1, PROJ_TM, d), lambda bi, i: (bi, i, 0)),
            pl.BlockSpec((1, d), lambda bi, i: (0, 0)),
            pl.BlockSpec((1, 1, d), lambda bi, i: (bi, 0, 0)),
            pl.BlockSpec((1, 1, d), lambda bi, i: (bi, 0, 0)),
            pl.BlockSpec((d, d_in), lambda bi, i: (0, 0)),
        ],
        out_specs=[
            pl.BlockSpec((1, PROJ_TM, 4 * D_RET), lambda bi, i: (bi, i, 0)),
            pl.BlockSpec((1, PROJ_TM, 3 * D_SB), lambda bi, i: (bi, i, 0)),
            pl.BlockSpec((1, PROJ_TM, D_SB), lambda bi, i: (bi, i, 0)),
        ],
        out_shape=[
            jax.ShapeDtypeStruct((b, s, 4 * D_RET), F32),
            jax.ShapeDtypeStruct((b, s, 3 * D_SB), BF16),
            jax.ShapeDtypeStruct((b, s, D_SB), F32),
        ],
        compiler_params=pltpu.CompilerParams(vmem_limit_bytes=56 * MIB),
        name="in_proj",
    )(x, g.reshape(1, d), scale.reshape(b, 1, d), shift.reshape(b, 1, d), w_in_bf16)


def _retention_tables(seq):
    half = RET_HEAD_DIM // 2
    pos = jnp.arange(seq, dtype=F32)
    inv = ROPE_BASE ** (-jnp.arange(half, dtype=F32) / half)
    ang = pos[:, None] * inv[None, :]
    cos, sin = jnp.cos(ang), jnp.sin(ang)
    cos2 = jnp.concatenate([cos, cos], axis=-1)
    sin2 = jnp.concatenate([-sin, sin], axis=-1)

    log_gamma = jnp.log1p(-(2.0 ** (-5.0 - jnp.arange(RET_HEADS, dtype=F32))))
    idx = jnp.arange(RET_L, dtype=F32)
    diff = idx[:, None] - idx[None, :]
    same = (jnp.arange(RET_L)[:, None] // CHUNK) == (jnp.arange(RET_L)[None, :] // CHUNK)
    expo = jnp.where(same, jnp.abs(diff), diff)
    dmask = jnp.where(same | (diff > 0), jnp.exp(expo[None] * log_gamma[:, None, None]), 0.0)
    q_dec = jnp.exp((idx + 1.0)[None, :] * log_gamma[:, None])
    k_dec = jnp.exp((RET_L - 1.0 - idx)[None, :] * log_gamma[:, None])
    blk_dec = jnp.exp(RET_L * log_gamma)
    rep = lambda t: jnp.broadcast_to(t[:, :, None], (RET_HEADS, RET_L, LANES))
    blk = jnp.broadcast_to(blk_dec[:, None, None], (RET_HEADS, 1, LANES))
    return cos2, sin2, dmask, rep(q_dec), rep(k_dec), blk


def _retention_kernel(q_ref, k_ref, v_ref, g_ref, cos_ref, sin_ref, dm_ref, qd_ref, kd_ref,
                      bd_ref, o_ref, state_ref):
    @pl.when(pl.program_id(1) == 0)
    def _():
        state_ref[...] = jnp.zeros_like(state_ref)

    cos = cos_ref[...]
    sin = sin_ref[...]
    half = RET_HEAD_DIM // 2
    for h in range(RET_HEADS):
        cols = slice(h * RET_HEAD_DIM, (h + 1) * RET_HEAD_DIM)
        q = q_ref[0, :, cols]
        k = k_ref[0, :, cols]
        q = q * cos + pltpu.roll(q, half, axis=1) * sin
        k = (k * cos + pltpu.roll(k, half, axis=1) * sin) * (RET_HEAD_DIM ** -0.5)
        vb = v_ref[0, :, cols].astype(BF16)
        qb = q.astype(BF16)
        scores = lax.dot_general(qb, k.astype(BF16), (((1,), (1,)), ((), ())),
                                 preferred_element_type=F32) * dm_ref[h]
        o = jnp.dot(scores.astype(BF16), vb, preferred_element_type=F32)
        state = state_ref[h]
        o = o + jnp.dot(qb, state.astype(BF16), preferred_element_type=F32) * qd_ref[h]
        kv = lax.dot_general((k * kd_ref[h]).astype(BF16), vb, (((0,), (0,)), ((), ())),
                             preferred_element_type=F32)
        state_ref[h] = state * bd_ref[h] + kv
        mu = jnp.mean(o, axis=-1, keepdims=True)
        oc = o - mu
        var = jnp.mean(oc * oc, axis=-1, keepdims=True)
        y = oc * lax.rsqrt(var + EPS) * _silu(g_ref[0, :, cols])
        o_ref[0, :, cols] = y.astype(o_ref.dtype)


def _retention(ret, tables):
    b, s, _ = ret.shape
    cos2, sin2, dmask, q_dec, k_dec, blk = tables
    col = lambda c: pl.BlockSpec((1, RET_L, D_RET), lambda bi, i: (bi, i, c))
    whole = lambda a: pl.BlockSpec(a.shape, lambda bi, i: (0,) * a.ndim)
    return pl.pallas_call(
        _retention_kernel,
        grid=(b, s // RET_L),
        in_specs=[
            col(0), col(1), col(2), col(3),
            pl.BlockSpec((RET_L, LANES), lambda bi, i: (i, 0)),
            pl.BlockSpec((RET_L, LANES), lambda bi, i: (i, 0)),
            whole(dmask), whole(q_dec), whole(k_dec), whole(blk),
        ],
        out_specs=pl.BlockSpec((1, RET_L, D_RET), lambda bi, i: (bi, i, 0)),
        out_shape=jax.ShapeDtypeStruct((b, s, D_RET), BF16),
        scratch_shapes=[pltpu.VMEM((RET_HEADS, RET_HEAD_DIM, RET_HEAD_DIM), F32)],
        name="retention",
    )(ret, ret, ret, ret, cos2, sin2, dmask, q_dec, k_dec, blk)


def _softplus2(u):
    return jnp.maximum(u, jnp.log(1.0 + jnp.exp2(jnp.minimum(u, SB_EXP2_CLAMP))) * LOG2E)


def _sb_scores(qm_ref, k_ref, u_ref, sp_ref, pair, slot, row0, key0, diagonal):
    rows = SB_TQ - row0
    lanes = slice(pair * LANES, (pair + 1) * LANES)
    qm = qm_ref[pair, :, row0:, :].reshape(2 * rows, LANES)
    kb = k_ref[0, pl.ds(key0, SB_TK), lanes]
    u = lax.dot_general(qm, kb, (((1,), (1,)), ((), ())), preferred_element_type=F32)
    sp = _softplus2(u.astype(BF16))
    if diagonal:
        t_loc = lax.broadcasted_iota(jnp.int32, (rows, SB_TK), 0)
        s_loc = lax.broadcasted_iota(jnp.int32, (rows, SB_TK), 1)
        m = s_loc < t_loc
        mask = jnp.concatenate([m, m], axis=0)
        sp = jnp.where(mask, sp, jnp.zeros_like(sp))
        u = jnp.where(mask, u, SB_MASKED)
    u_ref[pair, slot, :, row0:, :] = u.reshape(2, rows, SB_TK)
    sp_ref[pair, slot, :, row0:, :] = sp.reshape(2, rows, SB_TK)


def _sb_values(v_ref, tri_ref, u_ref, sp_ref, acc_ref, ncarry_ref, pair, slot, row0, key0):
    rows = SB_TQ - row0
    vb = v_ref[0, pl.ds(key0, SB_TK), pair * LANES:(pair + 1) * LANES]
    sp = sp_ref[pair, slot, :, row0:, :].reshape(2 * rows, SB_TK)
    r = jnp.dot(sp, tri_ref[...], preferred_element_type=F32)
    u = u_ref[pair, slot, :, row0:, :].reshape(2 * rows, SB_TK)
    ncarry = ncarry_ref[pair, :, row0:, :].reshape(2 * rows, LANES)
    a = jnp.exp2(u - r + jnp.concatenate([ncarry] * (SB_TK // LANES), axis=1))
    pv = jnp.dot(a.astype(BF16), vb, preferred_element_type=F32)
    acc = acc_ref[pair, :, row0:, :].reshape(2 * rows, LANES)
    acc_ref[pair, :, row0:, :] = (acc + pv).reshape(2, rows, LANES)
    ncarry_ref[pair, :, row0:, :] = (ncarry - r[:, :1]).reshape(2, rows, LANES)


def _sb_kernel(q_ref, k_ref, v_ref, g_ref, tri_ref, o_ref,
               qm_ref, acc_ref, ncarry_ref, u_ref, sp_ref):
    qi = pl.program_id(2)
    lane = lax.broadcasted_iota(jnp.int32, (SB_TQ, LANES), 1)
    first = lane < SB_HEAD_DIM
    for pair in range(SB_GROUP):
        q = q_ref[0, :, pair * LANES:(pair + 1) * LANES]
        zero = jnp.zeros_like(q)
        qm_ref[pair, 0] = jnp.where(first, q, zero)
        qm_ref[pair, 1] = jnp.where(first, zero, q)
    acc_ref[...] = jnp.zeros_like(acc_ref)
    ncarry_ref[...] = jnp.zeros_like(ncarry_ref)

    def scores(slot, row0, key0, diagonal):
        for pair in range(SB_GROUP):
            _sb_scores(qm_ref, k_ref, u_ref, sp_ref, pair, slot, row0, key0, diagonal)

    def values(slot, row0, key0):
        for pair in range(SB_GROUP):
            _sb_values(v_ref, tri_ref, u_ref, sp_ref, acc_ref, ncarry_ref, pair, slot, row0, key0)

    key_of = lambda blk: pl.multiple_of(blk * SB_TK, SB_TK)

    n_full = 2 * qi
    scores(0, SB_TK, key_of(n_full + 1), True)
    scores(1, 0, key_of(n_full), True)
    values(0, SB_TK, key_of(n_full + 1))

    def body(it, c):
        blk = n_full - 1 - 2 * it
        scores(0, 0, key_of(blk), False)
        values(1, 0, key_of(blk + 1))
        scores(1, 0, key_of(blk - 1), False)
        values(0, 0, key_of(blk))
        return c

    lax.fori_loop(0, qi, body, 0)
    values(1, 0, key_of(0))

    for pair in range(SB_GROUP):
        lanes = slice(pair * LANES, (pair + 1) * LANES)
        y = jnp.where(first, acc_ref[pair, 0], acc_ref[pair, 1]) * _silu(g_ref[0, :, lanes])
        o_ref[0, :, lanes] = y.astype(o_ref.dtype)


def _stick_breaking(sb, sg, tri):
    b, s, _ = sb.shape
    nq = s // SB_TQ
    groups = SB_PAIRS // SB_GROUP
    width = SB_GROUP * LANES
    return pl.pallas_call(
        _sb_kernel,
        grid=(b, groups, nq),
        in_specs=[
            pl.BlockSpec((1, SB_TQ, width), lambda bi, g, i: (bi, i, g)),
            pl.BlockSpec((1, s, width), lambda bi, g, i: (bi, 0, groups + g)),
            pl.BlockSpec((1, s, width), lambda bi, g, i: (bi, 0, 2 * groups + g)),
            pl.BlockSpec((1, SB_TQ, width), lambda bi, g, i: (bi, i, g)),
            pl.BlockSpec(tri.shape, lambda bi, g, i: (0, 0)),
        ],
        out_specs=pl.BlockSpec((1, SB_TQ, width), lambda bi, g, i: (bi, i, g)),
        out_shape=jax.ShapeDtypeStruct((b, s, D_SB), BF16),
        scratch_shapes=[
            pltpu.VMEM((SB_GROUP, 2, SB_TQ, LANES), BF16),
            pltpu.VMEM((SB_GROUP, 2, SB_TQ, LANES), F32),
            pltpu.VMEM((SB_GROUP, 2, SB_TQ, LANES), F32),
            pltpu.VMEM((SB_GROUP, 2, 2, SB_TQ, SB_TK), F32),
            pltpu.VMEM((SB_GROUP, 2, 2, SB_TQ, SB_TK), BF16),
        ],
        compiler_params=pltpu.CompilerParams(vmem_limit_bytes=48 * MIB),
        name="stick_breaking",
    )(sb, sb, sb, sg, tri)


def _outproj_kernel(yr_ref, ys_ref, w_ref, x_ref, gate_ref, fg_ref, o_ref, *, final):
    y = jnp.dot(yr_ref[0], w_ref[:D_RET, :], preferred_element_type=F32)
    y = y + jnp.dot(ys_ref[0], w_ref[D_RET:, :], preferred_element_type=F32)
    xn = x_ref[0] + gate_ref[0] * y
    if final:
        inv = lax.rsqrt(jnp.mean(xn * xn, axis=-1, keepdims=True) + EPS)
        xn = xn * inv * fg_ref[...]
    o_ref[0] = xn.astype(o_ref.dtype)


def _outproj(y_ret, y_sb, w_out_bf16, x, gate, final_g, final):
    b, s, d = x.shape
    return pl.pallas_call(
        functools.partial(_outproj_kernel, final=final),
        grid=(b, s // OUT_TM),
        in_specs=[
            pl.BlockSpec((1, OUT_TM, D_RET), lambda bi, i: (bi, i, 0)),
            pl.BlockSpec((1, OUT_TM, D_SB), lambda bi, i: (bi, i, 0)),
            pl.BlockSpec(w_out_bf16.shape, lambda bi, i: (0, 0)),
            pl.BlockSpec((1, OUT_TM, d), lambda bi, i: (bi, i, 0)),
            pl.BlockSpec((1, 1, d), lambda bi, i: (bi, 0, 0)),
            pl.BlockSpec((1, d), lambda bi, i: (0, 0)),
        ],
        out_specs=pl.BlockSpec((1, OUT_TM, d), lambda bi, i: (bi, i, 0)),
        out_shape=jax.ShapeDtypeStruct((b, s, d), x.dtype),
        name="out_proj_final" if final else "out_proj",
    )(y_ret, y_sb, w_out_bf16, x, gate.reshape(b, 1, d), final_g.reshape(1, d))


def kernel(x, c, norm_g, w_ada, b_ada, w_in, w_out, final_g):
    b, s, d = x.shape
    assert d == D_MODEL and s % max(SB_TQ, RET_L, PROJ_TM, OUT_TM) == 0 and SB_TQ == 2 * SB_TK
    c_pad = jnp.zeros((ADA_ROWS, d), F32).at[:b].set(c.astype(F32))
    mod = _ada(c_pad, w_ada, b_ada)[:, :b]
    tables = _retention_tables(s)
    tri = jnp.tril(jnp.ones((SB_TK, SB_TK), F32)).astype(BF16)
    w_in_b = w_in.astype(BF16)
    w_out_b = w_out.astype(BF16)

    h = x.astype(F32)
    for layer in range(DEPTH):
        shift, scale, gate = jnp.split(mod[layer], 3, axis=-1)
        ret, sb, sg = _inproj(h, norm_g[layer], scale, shift, w_in_b[layer])
        y_ret = _retention(ret, tables)
        y_sb = _stick_breaking(sb, sg, tri)
        h = _outproj(y_ret, y_sb, w_out_b[layer], h, gate, final_g, layer == DEPTH - 1)
    return h.astype(x.dtype)
```

```python
import functools

import jax
import jax.numpy as jnp
from jax import lax
from jax.experimental import pallas as pl
from jax.experimental.pallas import tpu as pltpu

D_MODEL = 1024
DEPTH = 2
CHUNK = 64
D_RET = 512
D_SB = 512
RET_HEADS = 4
RET_HEAD_DIM = 128
SB_HEADS = 8
SB_HEAD_DIM = 64
ROPE_BASE = 10000.0
EPS = 1e-6
LOG2E = 1.4426950408889634
LOG2E_HI = 1.4453125
LOG2E_LO = LOG2E - LOG2E_HI

LANES = 128
SB_PAIRS = SB_HEADS * SB_HEAD_DIM // LANES
SB_GROUP = 2
MIB = 1024 * 1024

ADA_ROWS = 8
ADA_TN = 768
PROJ_TM = 512
PROJ_TN = 512
RET_L = 256
SB_TQ = 512
SB_TK = 256
OUT_TM = 512

SB_EXP2_CLAMP = 126.0
SB_MASKED = -1e30
SB_Q_SCALE = LOG2E * SB_HEAD_DIM ** -0.5

F32 = jnp.float32
BF16 = jnp.bfloat16


def _silu(x):
    return x * (1.0 / (1.0 + jnp.exp(-x)))


def _ada_kernel(c_ref, w_ref, b_ref, o_ref):
    c_act = _silu(c_ref[...])
    o_ref[0] = jnp.dot(c_act, w_ref[0], preferred_element_type=F32,
                       precision=lax.Precision.HIGHEST) + b_ref[0]


def _ada(c_pad, w_ada, b_ada):
    n = 3 * D_MODEL
    return pl.pallas_call(
        _ada_kernel,
        grid=(DEPTH, n // ADA_TN),
        in_specs=[
            pl.BlockSpec((ADA_ROWS, D_MODEL), lambda l, j: (0, 0)),
            pl.BlockSpec((1, D_MODEL, ADA_TN), lambda l, j: (l, 0, j)),
            pl.BlockSpec((1, 1, ADA_TN), lambda l, j: (l, 0, j)),
        ],
        out_specs=pl.BlockSpec((1, ADA_ROWS, ADA_TN), lambda l, j: (l, 0, j)),
        out_shape=jax.ShapeDtypeStruct((DEPTH, ADA_ROWS, n), F32),
        name="ada_mod",
    )(c_pad, w_ada, b_ada.reshape(DEPTH, 1, n))


def _inproj_kernel(x_ref, g_ref, scale_ref, shift_ref, w_ref, ret_ref, sb_ref, sg_ref):
    x = x_ref[0]
    inv = lax.rsqrt(jnp.mean(x * x, axis=-1, keepdims=True) + EPS)
    h = (x * inv * g_ref[...]) * (1.0 + scale_ref[0]) + shift_ref[0]
    hb = h.astype(BF16)
    n_ret = 4 * D_RET // PROJ_TN
    n_qkv = 3 * D_SB // PROJ_TN
    for n in range(n_ret):
        cols = slice(n * PROJ_TN, (n + 1) * PROJ_TN)
        ret_ref[0, :, cols] = jnp.dot(hb, w_ref[:, cols], preferred_element_type=F32)
    for n in range(n_qkv):
        wcols = slice((n_ret + n) * PROJ_TN, (n_ret + n + 1) * PROJ_TN)
        p = jnp.dot(hb, w_ref[:, wcols], preferred_element_type=F32)
        if n * PROJ_TN < D_SB:
            p = p * SB_Q_SCALE
        sb_ref[0, :, n * PROJ_TN:(n + 1) * PROJ_TN] = p.astype(BF16)
    wcols = slice((n_ret + n_qkv) * PROJ_TN, (n_ret + n_qkv + 1) * PROJ_TN)
    sg_ref[0] = jnp.dot(hb, w_ref[:, wcols], preferred_element_type=F32)


def _inproj(x, g, scale, shift, w_in_bf16):
    b, s, d = x.shape
    d_in = w_in_bf16.shape[1]
    return pl.pallas_call(
        _inproj_kernel,
        grid=(b, s // PROJ_TM),
        in_specs=[
            pl.BlockSpec((1, PROJ_TM, d), lambda bi, i: (bi, i, 0)),
            pl.BlockSpec((1, d), lambda bi, i: (0, 0)),
            pl.BlockSpec((1, 1, d), lambda bi, i: (bi, 0, 0)),
            pl.BlockSpec((1, 1, d), lambda bi, i: (bi, 0, 0)),
            pl.BlockSpec((d, d_in), lambda bi, i: (0, 0)),
        ],
        out_specs=[
            pl.BlockSpec((1, PROJ_TM, 4 * D_RET), lambda bi, i: (bi, i, 0)),
            pl.BlockSpec((1, PROJ_TM, 3 * D_SB), lambda bi, i: (bi, i, 0)),
            pl.BlockSpec((1, PROJ_TM, D_SB), lambda bi, i: (bi, i, 0)),
        ],
        out_shape=[
            jax.ShapeDtypeStruct((b, s, 4 * D_RET), F32),
            jax.ShapeDtypeStruct((b, s, 3 * D_SB), BF16),
            jax.ShapeDtypeStruct((b, s, D_SB), F32),
        ],
        compiler_params=pltpu.CompilerParams(vmem_limit_bytes=56 * MIB),
        name="in_proj",
    )(x, g.reshape(1, d), scale.reshape(b, 1, d), shift.reshape(b, 1, d), w_in_bf16)


def _retention_tables(seq):
    half = RET_HEAD_DIM // 2
    pos = jnp.arange(seq, dtype=F32)
    inv = ROPE_BASE ** (-jnp.arange(half, dtype=F32) / half)
    ang = pos[:, None] * inv[None, :]
    cos, sin = jnp.cos(ang), jnp.sin(ang)
    cos2 = jnp.concatenate([cos, cos], axis=-1)
    sin2 = jnp.concatenate([-sin, sin], axis=-1)

    log_gamma = jnp.log1p(-(2.0 ** (-5.0 - jnp.arange(RET_HEADS, dtype=F32))))
    idx = jnp.arange(RET_L, dtype=F32)
    diff = idx[:, None] - idx[None, :]
    same = (jnp.arange(RET_L)[:, None] // CHUNK) == (jnp.arange(RET_L)[None, :] // CHUNK)
    expo = jnp.where(same, jnp.abs(diff), diff)
    dmask = jnp.where(same | (diff > 0), jnp.exp(expo[None] * log_gamma[:, None, None]), 0.0)
    q_dec = jnp.exp((idx + 1.0)[None, :] * log_gamma[:, None])
    k_dec = jnp.exp((RET_L - 1.0 - idx)[None, :] * log_gamma[:, None])
    blk_dec = jnp.exp(RET_L * log_gamma)
    rep = lambda t: jnp.broadcast_to(t[:, :, None], (RET_HEADS, RET_L, LANES))
    blk = jnp.broadcast_to(blk_dec[:, None, None], (RET_HEADS, 1, LANES))
    return cos2, sin2, dmask, rep(q_dec), rep(k_dec), blk


def _retention_kernel(q_ref, k_ref, v_ref, g_ref, cos_ref, sin_ref, dm_ref, qd_ref, kd_ref,
                      bd_ref, o_ref, state_ref):
    @pl.when(pl.program_id(1) == 0)
    def _():
        state_ref[...] = jnp.zeros_like(state_ref)

    cos = cos_ref[...]
    sin = sin_ref[...]
    half = RET_HEAD_DIM // 2
    for h in range(RET_HEADS):
        cols = slice(h * RET_HEAD_DIM, (h + 1) * RET_HEAD_DIM)
        q = q_ref[0, :, cols]
        k = k_ref[0, :, cols]
        q = q * cos + pltpu.roll(q, half, axis=1) * sin
        k = (k * cos + pltpu.roll(k, half, axis=1) * sin) * (RET_HEAD_DIM ** -0.5)
        vb = v_ref[0, :, cols].astype(BF16)
        qb = q.astype(BF16)
        scores = lax.dot_general(qb, k.astype(BF16), (((1,), (1,)), ((), ())),
                                 preferred_element_type=F32) * dm_ref[h]
        o = jnp.dot(scores.astype(BF16), vb, preferred_element_type=F32)
        state = state_ref[h]
        o = o + jnp.dot(qb, state.astype(BF16), preferred_element_type=F32) * qd_ref[h]
        kv = lax.dot_general((k * kd_ref[h]).astype(BF16), vb, (((0,), (0,)), ((), ())),
                             preferred_element_type=F32)
        state_ref[h] = state * bd_ref[h] + kv
        mu = jnp.mean(o, axis=-1, keepdims=True)
        oc = o - mu
        var = jnp.mean(oc * oc, axis=-1, keepdims=True)
        y = oc * lax.rsqrt(var + EPS) * _silu(g_ref[0, :, cols])
        o_ref[0, :, cols] = y.astype(o_ref.dtype)


def _retention(ret, tables):
    b, s, _ = ret.shape
    cos2, sin2, dmask, q_dec, k_dec, blk = tables
    col = lambda c: pl.BlockSpec((1, RET_L, D_RET), lambda bi, i: (bi, i, c))
    whole = lambda a: pl.BlockSpec(a.shape, lambda bi, i: (0,) * a.ndim)
    return pl.pallas_call(
        _retention_kernel,
        grid=(b, s // RET_L),
        in_specs=[
            col(0), col(1), col(2), col(3),
            pl.BlockSpec((RET_L, LANES), lambda bi, i: (i, 0)),
            pl.BlockSpec((RET_L, LANES), lambda bi, i: (i, 0)),
            whole(dmask), whole(q_dec), whole(k_dec), whole(blk),
        ],
        out_specs=pl.BlockSpec((1, RET_L, D_RET), lambda bi, i: (bi, i, 0)),
        out_shape=jax.ShapeDtypeStruct((b, s, D_RET), BF16),
        scratch_shapes=[pltpu.VMEM((RET_HEADS, RET_HEAD_DIM, RET_HEAD_DIM), F32)],
        name="retention",
    )(ret, ret, ret, ret, cos2, sin2, dmask, q_dec, k_dec, blk)


def _softplus2(u):
    ln = jnp.log(1.0 + jnp.exp2(jnp.minimum(u, SB_EXP2_CLAMP)))
    return jnp.maximum(u, ln * LOG2E_HI + ln * LOG2E_LO)


def _sb_scores(qm_ref, k_ref, u_ref, sp_ref, pair, slot, row0, key0, diagonal):
    rows = SB_TQ - row0
    lanes = slice(pair * LANES, (pair + 1) * LANES)
    qm = qm_ref[pair, :, row0:, :].reshape(2 * rows, LANES)
    kb = k_ref[0, pl.ds(key0, SB_TK), lanes]
    u = lax.dot_general(qm, kb, (((1,), (1,)), ((), ())), preferred_element_type=F32)
    sp = _softplus2(u.astype(BF16))
    if diagonal:
        t_loc = lax.broadcasted_iota(jnp.int32, (rows, SB_TK), 0)
        s_loc = lax.broadcasted_iota(jnp.int32, (rows, SB_TK), 1)
        m = s_loc < t_loc
        mask = jnp.concatenate([m, m], axis=0)
        sp = jnp.where(mask, sp, jnp.zeros_like(sp))
        u = jnp.where(mask, u, SB_MASKED)
    u_ref[pair, slot, :, row0:, :] = u.reshape(2, rows, SB_TK)
    sp_ref[pair, slot, :, row0:, :] = sp.reshape(2, rows, SB_TK)


def _sb_values(v_ref, tri_ref, u_ref, sp_ref, acc_ref, ncarry_ref, pair, slot, row0, key0):
    rows = SB_TQ - row0
    vb = v_ref[0, pl.ds(key0, SB_TK), pair * LANES:(pair + 1) * LANES]
    sp = sp_ref[pair, slot, :, row0:, :].reshape(2 * rows, SB_TK)
    r = jnp.dot(sp, tri_ref[...], preferred_element_type=F32)
    u = u_ref[pair, slot, :, row0:, :].reshape(2 * rows, SB_TK)
    ncarry = ncarry_ref[pair, :, row0:, :].reshape(2 * rows, LANES)
    a = jnp.exp2(u - r + jnp.concatenate([ncarry] * (SB_TK // LANES), axis=1))
    pv = jnp.dot(a.astype(BF16), vb, preferred_element_type=F32)
    acc = acc_ref[pair, :, row0:, :].reshape(2 * rows, LANES)
    acc_ref[pair, :, row0:, :] = (acc + pv).reshape(2, rows, LANES)
    ncarry_ref[pair, :, row0:, :] = (ncarry - r[:, :1]).reshape(2, rows, LANES)


def _sb_kernel(q_ref, k_ref, v_ref, g_ref, tri_ref, o_ref,
               qm_ref, acc_ref, ncarry_ref, u_ref, sp_ref):
    qi = pl.program_id(2)
    lane = lax.broadcasted_iota(jnp.int32, (SB_TQ, LANES), 1)
    first = lane < SB_HEAD_DIM
    for pair in range(SB_GROUP):
        q = q_ref[0, :, pair * LANES:(pair + 1) * LANES]
        zero = jnp.zeros_like(q)
        qm_ref[pair, 0] = jnp.where(first, q, zero)
        qm_ref[pair, 1] = jnp.where(first, zero, q)
    acc_ref[...] = jnp.zeros_like(acc_ref)
    ncarry_ref[...] = jnp.zeros_like(ncarry_ref)

    def scores(slot, row0, key0, diagonal):
        for pair in range(SB_GROUP):
            _sb_scores(qm_ref, k_ref, u_ref, sp_ref, pair, slot, row0, key0, diagonal)

    def values(slot, row0, key0):
        for pair in range(SB_GROUP):
            _sb_values(v_ref, tri_ref, u_ref, sp_ref, acc_ref, ncarry_ref, pair, slot, row0, key0)

    key_of = lambda blk: pl.multiple_of(blk * SB_TK, SB_TK)

    n_full = 2 * qi
    scores(0, SB_TK, key_of(n_full + 1), True)
    scores(1, 0, key_of(n_full), True)
    values(0, SB_TK, key_of(n_full + 1))

    def body(it, c):
        blk = n_full - 1 - 2 * it
        scores(0, 0, key_of(blk), False)
        values(1, 0, key_of(blk + 1))
        scores(1, 0, key_of(blk - 1), False)
        values(0, 0, key_of(blk))
        return c

    lax.fori_loop(0, qi, body, 0)
    values(1, 0, key_of(0))

    for pair in range(SB_GROUP):
        lanes = slice(pair * LANES, (pair + 1) * LANES)
        y = jnp.where(first, acc_ref[pair, 0], acc_ref[pair, 1]) * _silu(g_ref[0, :, lanes])
        o_ref[0, :, lanes] = y.astype(o_ref.dtype)


def _stick_breaking(sb, sg, tri):
    b, s, _ = sb.shape
    nq = s // SB_TQ
    groups = SB_PAIRS // SB_GROUP
    width = SB_GROUP * LANES
    return pl.pallas_call(
        _sb_kernel,
        grid=(b, groups, nq),
        in_specs=[
            pl.BlockSpec((1, SB_TQ, width), lambda bi, g, i: (bi, i, g)),
            pl.BlockSpec((1, s, width), lambda bi, g, i: (bi, 0, groups + g)),
            pl.BlockSpec((1, s, width), lambda bi, g, i: (bi, 0, 2 * groups + g)),
            pl.BlockSpec((1, SB_TQ, width), lambda bi, g, i: (bi, i, g)),
            pl.BlockSpec(tri.shape, lambda bi, g, i: (0, 0)),
        ],
        out_specs=pl.BlockSpec((1, SB_TQ, width), lambda bi, g, i: (bi, i, g)),
        out_shape=jax.ShapeDtypeStruct((b, s, D_SB), BF16),
        scratch_shapes=[
            pltpu.VMEM((SB_GROUP, 2, SB_TQ, LANES), BF16),
            pltpu.VMEM((SB_GROUP, 2, SB_TQ, LANES), F32),
            pltpu.VMEM((SB_GROUP, 2, SB_TQ, LANES), F32),
            pltpu.VMEM((SB_GROUP, 2, 2, SB_TQ, SB_TK), F32),
            pltpu.VMEM((SB_GROUP, 2, 2, SB_TQ, SB_TK), BF16),
        ],
        compiler_params=pltpu.CompilerParams(vmem_limit_bytes=48 * MIB),
        name="stick_breaking",
    )(sb, sb, sb, sg, tri)


def _outproj_kernel(yr_ref, ys_ref, w_ref, x_ref, gate_ref, fg_ref, o_ref, *, final):
    y = jnp.dot(yr_ref[0], w_ref[:D_RET, :], preferred_element_type=F32)
    y = y + jnp.dot(ys_ref[0], w_ref[D_RET:, :], preferred_element_type=F32)
    xn = x_ref[0] + gate_ref[0] * y
    if final:
        inv = lax.rsqrt(jnp.mean(xn * xn, axis=-1, keepdims=True) + EPS)
        xn = xn * inv * fg_ref[...]
    o_ref[0] = xn.astype(o_ref.dtype)


def _outproj(y_ret, y_sb, w_out_bf16, x, gate, final_g, final):
    b, s, d = x.shape
    return pl.pallas_call(
        functools.partial(_outproj_kernel, final=final),
        grid=(b, s // OUT_TM),
        in_specs=[
            pl.BlockSpec((1, OUT_TM, D_RET), lambda bi, i: (bi, i, 0)),
            pl.BlockSpec((1, OUT_TM, D_SB), lambda bi, i: (bi, i, 0)),
            pl.BlockSpec(w_out_bf16.shape, lambda bi, i: (0, 0)),
            pl.BlockSpec((1, OUT_TM, d), lambda bi, i: (bi, i, 0)),
            pl.BlockSpec((1, 1, d), lambda bi, i: (bi, 0, 0)),
            pl.BlockSpec((1, d), lambda bi, i: (0, 0)),
        ],
        out_specs=pl.BlockSpec((1, OUT_TM, d), lambda bi, i: (bi, i, 0)),
        out_shape=jax.ShapeDtypeStruct((b, s, d), x.dtype),
        name="out_proj_final" if final else "out_proj",
    )(y_ret, y_sb, w_out_bf16, x, gate.reshape(b, 1, d), final_g.reshape(1, d))


def kernel(x, c, norm_g, w_ada, b_ada, w_in, w_out, final_g):
    b, s, d = x.shape
    assert d == D_MODEL and s % max(SB_TQ, RET_L, PROJ_TM, OUT_TM) == 0 and SB_TQ == 2 * SB_TK
    c_pad = jnp.zeros((ADA_ROWS, d), F32).at[:b].set(c.astype(F32))
    mod = _ada(c_pad, w_ada, b_ada)[:, :b]
    tables = _retention_tables(s)
    tri = jnp.tril(jnp.ones((SB_TK, SB_TK), F32)).astype(BF16)
    w_in_b = w_in.astype(BF16)
    w_out_b = w_out.astype(BF16)

    h = x.astype(F32)
    for layer in range(DEPTH):
        shift, scale, gate = jnp.split(mod[layer], 3, axis=-1)
        ret, sb, sg = _inproj(h, norm_g[layer], scale, shift, w_in_b[layer])
        y_ret = _retention(ret, tables)
        y_sb = _stick_breaking(sb, sg, tri)
        h = _outproj(y_ret, y_sb, w_out_b[layer], h, gate, final_g, layer == DEPTH - 1)
    return h.astype(x.dtype)
```

```python
import functools

import jax
import jax.numpy as jnp
from jax import lax
from jax.experimental import pallas as pl
from jax.experimental.pallas import tpu as pltpu

D_MODEL = 1024
DEPTH = 2
CHUNK = 64
D_RET = 512
D_SB = 512
RET_HEADS = 4
RET_HEAD_DIM = 128
SB_HEADS = 8
SB_HEAD_DIM = 64
ROPE_BASE = 10000.0
EPS = 1e-6
LOG2E = 1.4426950408889634
LOG2E_HI = 1.4453125
LOG2E_LO = LOG2E - LOG2E_HI

LANES = 128
SB_PAIRS = SB_HEADS * SB_HEAD_DIM // LANES
SB_GROUP = 2
MIB = 1024 * 1024

ADA_ROWS = 8
ADA_TN = 768
PROJ_TM = 512
PROJ_TN = 512
RET_L = 256
SB_TQ = 512
SB_TK = 256
OUT_TM = 512

SB_EXP2_CLAMP = 126.0
SB_DEAD = 151.0
SB_MASKED = -1e30
SB_Q_SCALE = LOG2E * SB_HEAD_DIM ** -0.5

F32 = jnp.float32
BF16 = jnp.bfloat16


def _silu(x):
    return x * (1.0 / (1.0 + jnp.exp(-x)))


def _ada_kernel(c_ref, w_ref, b_ref, o_ref):
    c_act = _silu(c_ref[...])
    o_ref[0] = jnp.dot(c_act, w_ref[0], preferred_element_type=F32,
                       precision=lax.Precision.HIGHEST) + b_ref[0]


def _ada(c_pad, w_ada, b_ada):
    n = 3 * D_MODEL
    return pl.pallas_call(
        _ada_kernel,
        grid=(DEPTH, n // ADA_TN),
        in_specs=[
            pl.BlockSpec((ADA_ROWS, D_MODEL), lambda l, j: (0, 0)),
            pl.BlockSpec((1, D_MODEL, ADA_TN), lambda l, j: (l, 0, j)),
            pl.BlockSpec((1, 1, ADA_TN), lambda l, j: (l, 0, j)),
        ],
        out_specs=pl.BlockSpec((1, ADA_ROWS, ADA_TN), lambda l, j: (l, 0, j)),
        out_shape=jax.ShapeDtypeStruct((DEPTH, ADA_ROWS, n), F32),
        name="ada_mod",
    )(c_pad, w_ada, b_ada.reshape(DEPTH, 1, n))


def _inproj_kernel(x_ref, g_ref, scale_ref, shift_ref, w_ref, ret_ref, sb_ref, sg_ref):
    x = x_ref[0]
    inv = lax.rsqrt(jnp.mean(x * x, axis=-1, keepdims=True) + EPS)
    h = (x * inv * g_ref[...]) * (1.0 + scale_ref[0]) + shift_ref[0]
    hb = h.astype(BF16)
    n_ret = 4 * D_RET // PROJ_TN
    n_qkv = 3 * D_SB // PROJ_TN
    for n in range(n_ret):
        cols = slice(n * PROJ_TN, (n + 1) * PROJ_TN)
        ret_ref[0, :, cols] = jnp.dot(hb, w_ref[:, cols], preferred_element_type=F32)
    for n in range(n_qkv):
        wcols = slice((n_ret + n) * PROJ_TN, (n_ret + n + 1) * PROJ_TN)
        p = jnp.dot(hb, w_ref[:, wcols], preferred_element_type=F32)
        if n * PROJ_TN < D_SB:
            p = p * SB_Q_SCALE
        sb_ref[0, :, n * PROJ_TN:(n + 1) * PROJ_TN] = p.astype(BF16)
    wcols = slice((n_ret + n_qkv) * PROJ_TN, (n_ret + n_qkv + 1) * PROJ_TN)
    sg_ref[0] = jnp.dot(hb, w_ref[:, wcols], preferred_element_type=F32)


def _inproj(x, g, scale, shift, w_in_bf16):
    b, s, d = x.shape
    d_in = w_in_bf16.shape[1]
    return pl.pallas_call(
        _inproj_kernel,
        grid=(b, s // PROJ_TM),
        in_specs=[
            pl.BlockSpec((1, PROJ_TM, d), lambda bi, i: (bi, i, 0)),
            pl.BlockSpec((1, d), lambda bi, i: (0, 0)),
            pl.BlockSpec((1, 1, d), lambda bi, i: (bi, 0, 0)),
            pl.BlockSpec((1, 1, d), lambda bi, i: (bi, 0, 0)),
            pl.BlockSpec((d, d_in), lambda bi, i: (0, 0)),
        ],
        out_specs=[
            pl.BlockSpec((1, PROJ_TM, 4 * D_RET), lambda bi, i: (bi, i, 0)),
            pl.BlockSpec((1, PROJ_TM, 3 * D_SB), lambda bi, i: (bi, i, 0)),
            pl.BlockSpec((1, PROJ_TM, D_SB), lambda bi, i: (bi, i, 0)),
        ],
        out_shape=[
            jax.ShapeDtypeStruct((b, s, 4 * D_RET), F32),
            jax.ShapeDtypeStruct((b, s, 3 * D_SB), BF16),
            jax.ShapeDtypeStruct((b, s, D_SB), F32),
        ],
        compiler_params=pltpu.CompilerParams(vmem_limit_bytes=56 * MIB),
        name="in_proj",
    )(x, g.reshape(1, d), scale.reshape(b, 1, d), shift.reshape(b, 1, d), w_in_bf16)


def _retention_tables(seq):
    half = RET_HEAD_DIM // 2
    pos = jnp.arange(seq, dtype=F32)
    inv = ROPE_BASE ** (-jnp.arange(half, dtype=F32) / half)
    ang = pos[:, None] * inv[None, :]
    cos, sin = jnp.cos(ang), jnp.sin(ang)
    cos2 = jnp.concatenate([cos, cos], axis=-1)
    sin2 = jnp.concatenate([-sin, sin], axis=-1)

    log_gamma = jnp.log1p(-(2.0 ** (-5.0 - jnp.arange(RET_HEADS, dtype=F32))))
    idx = jnp.arange(RET_L, dtype=F32)
    diff = idx[:, None] - idx[None, :]
    same = (jnp.arange(RET_L)[:, None] // CHUNK) == (jnp.arange(RET_L)[None, :] // CHUNK)
    expo = jnp.where(same, jnp.abs(diff), diff)
    dmask = jnp.where(same | (diff > 0), jnp.exp(expo[None] * log_gamma[:, None, None]), 0.0)
    q_dec = jnp.exp((idx + 1.0)[None, :] * log_gamma[:, None])
    k_dec = jnp.exp((RET_L - 1.0 - idx)[None, :] * log_gamma[:, None])
    blk_dec = jnp.exp(RET_L * log_gamma)
    rep = lambda t: jnp.broadcast_to(t[:, :, None], (RET_HEADS, RET_L, LANES))
    blk = jnp.broadcast_to(blk_dec[:, None, None], (RET_HEADS, 1, LANES))
    return cos2, sin2, dmask, rep(q_dec), rep(k_dec), blk


def _retention_kernel(q_ref, k_ref, v_ref, g_ref, cos_ref, sin_ref, dm_ref, qd_ref, kd_ref,
                      bd_ref, o_ref, state_ref):
    @pl.when(pl.program_id(1) == 0)
    def _():
        state_ref[...] = jnp.zeros_like(state_ref)

    cos = cos_ref[...]
    sin = sin_ref[...]
    half = RET_HEAD_DIM // 2
    for h in range(RET_HEADS):
        cols = slice(h * RET_HEAD_DIM, (h + 1) * RET_HEAD_DIM)
        q = q_ref[0, :, cols]
        k = k_ref[0, :, cols]
        q = q * cos + pltpu.roll(q, half, axis=1) * sin
        k = (k * cos + pltpu.roll(k, half, axis=1) * sin) * (RET_HEAD_DIM ** -0.5)
        vb = v_ref[0, :, cols].astype(BF16)
        qb = q.astype(BF16)
        scores = lax.dot_general(qb, k.astype(BF16), (((1,), (1,)), ((), ())),
                                 preferred_element_type=F32) * dm_ref[h]
        o = jnp.dot(scores.astype(BF16), vb, preferred_element_type=F32)
        state = state_ref[h]
        o = o + jnp.dot(qb, state.astype(BF16), preferred_element_type=F32) * qd_ref[h]
        kv = lax.dot_general((k * kd_ref[h]).astype(BF16), vb, (((0,), (0,)), ((), ())),
                             preferred_element_type=F32)
        state_ref[h] = state * bd_ref[h] + kv
        mu = jnp.mean(o, axis=-1, keepdims=True)
        oc = o - mu
        var = jnp.mean(oc * oc, axis=-1, keepdims=True)
        y = oc * lax.rsqrt(var + EPS) * _silu(g_ref[0, :, cols])
        o_ref[0, :, cols] = y.astype(o_ref.dtype)


def _retention(ret, tables):
    b, s, _ = ret.shape
    cos2, sin2, dmask, q_dec, k_dec, blk = tables
    col = lambda c: pl.BlockSpec((1, RET_L, D_RET), lambda bi, i: (bi, i, c))
    whole = lambda a: pl.BlockSpec(a.shape, lambda bi, i: (0,) * a.ndim)
    return pl.pallas_call(
        _retention_kernel,
        grid=(b, s // RET_L),
        in_specs=[
            col(0), col(1), col(2), col(3),
            pl.BlockSpec((RET_L, LANES), lambda bi, i: (i, 0)),
            pl.BlockSpec((RET_L, LANES), lambda bi, i: (i, 0)),
            whole(dmask), whole(q_dec), whole(k_dec), whole(blk),
        ],
        out_specs=pl.BlockSpec((1, RET_L, D_RET), lambda bi, i: (bi, i, 0)),
        out_shape=jax.ShapeDtypeStruct((b, s, D_RET), BF16),
        scratch_shapes=[pltpu.VMEM((RET_HEADS, RET_HEAD_DIM, RET_HEAD_DIM), F32)],
        name="retention",
    )(ret, ret, ret, ret, cos2, sin2, dmask, q_dec, k_dec, blk)


def _softplus2(u):
    ln = jnp.log(1.0 + jnp.exp2(jnp.minimum(u, SB_EXP2_CLAMP)))
    return jnp.maximum(u, ln * LOG2E_HI + ln * LOG2E_LO)


def _sb_scores(qm_ref, k_ref, u_ref, sp_ref, pair, slot, row0, key0, diagonal):
    rows = SB_TQ - row0
    kb = k_ref[0, pl.ds(key0, SB_TK), pair * LANES:(pair + 1) * LANES]
    for head in range(2):
        u = lax.dot_general(qm_ref[pair, head, row0:, :], kb, (((1,), (1,)), ((), ())),
                            preferred_element_type=F32)
        sp = _softplus2(u.astype(BF16))
        if diagonal:
            t_loc = lax.broadcasted_iota(jnp.int32, (rows, SB_TK), 0)
            s_loc = lax.broadcasted_iota(jnp.int32, (rows, SB_TK), 1)
            mask = s_loc < t_loc
            sp = jnp.where(mask, sp, jnp.zeros_like(sp))
            u = jnp.where(mask, u, SB_MASKED)
        u_ref[pair, slot, head, row0:, :] = u
        sp_ref[pair, slot, head, row0:, :] = sp


def _sb_values(v_ref, tri_ref, u_ref, sp_ref, acc_ref, ncarry_ref, pair, slot, row0, key0):
    vb = v_ref[0, pl.ds(key0, SB_TK), pair * LANES:(pair + 1) * LANES]
    for head in range(2):
        r = jnp.dot(sp_ref[pair, slot, head, row0:, :], tri_ref[...], preferred_element_type=F32)
        ncarry = ncarry_ref[pair, head, row0:, :]
        a = jnp.exp2(u_ref[pair, slot, head, row0:, :] - r
                     + jnp.concatenate([ncarry] * (SB_TK // LANES), axis=1))
        acc_ref[pair, head, row0:, :] += jnp.dot(a.astype(BF16), vb, preferred_element_type=F32)
        ncarry_ref[pair, head, row0:, :] = ncarry - r[:, :1]


def _sb_kernel(q_ref, k_ref, v_ref, g_ref, tri_ref, o_ref,
               qm_ref, acc_ref, ncarry_ref, u_ref, sp_ref):
    qi = pl.program_id(2)
    lane = lax.broadcasted_iota(jnp.int32, (SB_TQ, LANES), 1)
    first = lane < SB_HEAD_DIM
    for pair in range(SB_GROUP):
        q = q_ref[0, :, pair * LANES:(pair + 1) * LANES]
        zero = jnp.zeros_like(q)
        qm_ref[pair, 0] = jnp.where(first, q, zero)
        qm_ref[pair, 1] = jnp.where(first, zero, q)
    acc_ref[...] = jnp.zeros_like(acc_ref)
    ncarry_ref[...] = jnp.zeros_like(ncarry_ref)

    def scores(slot, row0, key0, diagonal):
        for pair in range(SB_GROUP):
            _sb_scores(qm_ref, k_ref, u_ref, sp_ref, pair, slot, row0, key0, diagonal)

    def values(slot, row0, key0):
        for pair in range(SB_GROUP):
            _sb_values(v_ref, tri_ref, u_ref, sp_ref, acc_ref, ncarry_ref, pair, slot, row0, key0)

    key_of = lambda blk: pl.multiple_of(blk * SB_TK, SB_TK)

    n_full = 2 * qi
    scores(0, SB_TK, key_of(n_full + 1), True)
    scores(1, 0, key_of(n_full), True)
    values(0, SB_TK, key_of(n_full + 1))

    def body(state):
        it, _ = state
        blk = n_full - 1 - 2 * it
        scores(0, 0, key_of(blk), False)
        values(1, 0, key_of(blk + 1))
        scores(1, 0, key_of(blk - 1), False)
        values(0, 0, key_of(blk))
        live = jnp.max(ncarry_ref[...]) > -SB_DEAD
        return it + 1, live.astype(jnp.int32)

    _, live = lax.while_loop(lambda state: jnp.logical_and(state[0] < qi, state[1] > 0),
                             body, (jnp.int32(0), jnp.int32(1)))

    @pl.when(live > 0)
    def _():
        values(1, 0, key_of(0))

    for pair in range(SB_GROUP):
        lanes = slice(pair * LANES, (pair + 1) * LANES)
        y = jnp.where(first, acc_ref[pair, 0], acc_ref[pair, 1]) * _silu(g_ref[0, :, lanes])
        o_ref[0, :, lanes] = y.astype(o_ref.dtype)


def _stick_breaking(sb, sg, tri):
    b, s, _ = sb.shape
    nq = s // SB_TQ
    groups = SB_PAIRS // SB_GROUP
    width = SB_GROUP * LANES
    return pl.pallas_call(
        _sb_kernel,
        grid=(b, groups, nq),
        in_specs=[
            pl.BlockSpec((1, SB_TQ, width), lambda bi, g, i: (bi, i, g)),
            pl.BlockSpec((1, s, width), lambda bi, g, i: (bi, 0, groups + g)),
            pl.BlockSpec((1, s, width), lambda bi, g, i: (bi, 0, 2 * groups + g)),
            pl.BlockSpec((1, SB_TQ, width), lambda bi, g, i: (bi, i, g)),
            pl.BlockSpec(tri.shape, lambda bi, g, i: (0, 0)),
        ],
        out_specs=pl.BlockSpec((1, SB_TQ, width), lambda bi, g, i: (bi, i, g)),
        out_shape=jax.ShapeDtypeStruct((b, s, D_SB), BF16),
        scratch_shapes=[
            pltpu.VMEM((SB_GROUP, 2, SB_TQ, LANES), BF16),
            pltpu.VMEM((SB_GROUP, 2, SB_TQ, LANES), F32),
            pltpu.VMEM((SB_GROUP, 2, SB_TQ, LANES), F32),
            pltpu.VMEM((SB_GROUP, 2, 2, SB_TQ, SB_TK), F32),
            pltpu.VMEM((SB_GROUP, 2, 2, SB_TQ, SB_TK), BF16),
        ],
        compiler_params=pltpu.CompilerParams(vmem_limit_bytes=48 * MIB),
        name="stick_breaking",
    )(sb, sb, sb, sg, tri)


def _outproj_kernel(yr_ref, ys_ref, w_ref, x_ref, gate_ref, fg_ref, o_ref, *, final):
    y = jnp.dot(yr_ref[0], w_ref[:D_RET, :], preferred_element_type=F32)
    y = y + jnp.dot(ys_ref[0], w_ref[D_RET:, :], preferred_element_type=F32)
    xn = x_ref[0] + gate_ref[0] * y
    if final:
        inv = lax.rsqrt(jnp.mean(xn * xn, axis=-1, keepdims=True) + EPS)
        xn = xn * inv * fg_ref[...]
    o_ref[0] = xn.astype(o_ref.dtype)


def _outproj(y_ret, y_sb, w_out_bf16, x, gate, final_g, final):
    b, s, d = x.shape
    return pl.pallas_call(
        functools.partial(_outproj_kernel, final=final),
        grid=(b, s // OUT_TM),
        in_specs=[
            pl.BlockSpec((1, OUT_TM, D_RET), lambda bi, i: (bi, i, 0)),
            pl.BlockSpec((1, OUT_TM, D_SB), lambda bi, i: (bi, i, 0)),
            pl.BlockSpec(w_out_bf16.shape, lambda bi, i: (0, 0)),
            pl.BlockSpec((1, OUT_TM, d), lambda bi, i: (bi, i, 0)),
            pl.BlockSpec((1, 1, d), lambda bi, i: (bi, 0, 0)),
            pl.BlockSpec((1, d), lambda bi, i: (0, 0)),
        ],
        out_specs=pl.BlockSpec((1, OUT_TM, d), lambda bi, i: (bi, i, 0)),
        out_shape=jax.ShapeDtypeStruct((b, s, d), x.dtype),
        name="out_proj_final" if final else "out_proj",
    )(y_ret, y_sb, w_out_bf16, x, gate.reshape(b, 1, d), final_g.reshape(1, d))


def kernel(x, c, norm_g, w_ada, b_ada, w_in, w_out, final_g):
    b, s, d = x.shape
    assert d == D_MODEL and s % max(SB_TQ, RET_L, PROJ_TM, OUT_TM) == 0 and SB_TQ == 2 * SB_TK
    c_pad = jnp.zeros((ADA_ROWS, d), F32).at[:b].set(c.astype(F32))
    mod = _ada(c_pad, w_ada, b_ada)[:, :b]
    tables = _retention_tables(s)
    tri = jnp.tril(jnp.ones((SB_TK, SB_TK), F32)).astype(BF16)
    w_in_b = w_in.astype(BF16)
    w_out_b = w_out.astype(BF16)

    h = x.astype(F32)
    for layer in range(DEPTH):
        shift, scale, gate = jnp.split(mod[layer], 3, axis=-1)
        ret, sb, sg = _inproj(h, norm_g[layer], scale, shift, w_in_b[layer])
        y_ret = _retention(ret, tables)
        y_sb = _stick_breaking(sb, sg, tri)
        h = _outproj(y_ret, y_sb, w_out_b[layer], h, gate, final_g, layer == DEPTH - 1)
    return h.astype(x.dtype)
```

```python
import functools

import jax
import jax.numpy as jnp
import numpy as np
from jax import lax
from jax.experimental import pallas as pl
from jax.experimental.pallas import tpu as pltpu

D_MODEL = 1024
DEPTH = 2
CHUNK = 64
D_RET = 512
D_SB = 512
RET_HEADS = 4
RET_HEAD_DIM = 128
SB_HEADS = 8
SB_HEAD_DIM = 64
ROPE_BASE = 10000.0
EPS = 1e-6
LOG2E = 1.4426950408889634
LOG2E_HI = 1.4453125
LOG2E_LO = LOG2E - LOG2E_HI

LANES = 128
SB_PAIRS = SB_HEADS * SB_HEAD_DIM // LANES
MIB = 1024 * 1024

ADA_ROWS = 8
ADA_TN = 768
PROJ_TM = 512
PROJ_TN = 512
RET_L = 256
SB_TK = 256
SB_TQ = SB_TK
OUT_TM = 512

SB_EXP2_CLAMP = 126.0
SB_DEAD = 151.0
SB_MASKED = -1e30
SB_Q_SCALE = LOG2E * SB_HEAD_DIM ** -0.5

F32 = jnp.float32
BF16 = jnp.bfloat16


def _silu(x):
    return x * (1.0 / (1.0 + jnp.exp(-x)))


def _ada_kernel(c_ref, w_ref, b_ref, o_ref):
    c_act = _silu(c_ref[...])
    o_ref[0] = jnp.dot(c_act, w_ref[0], preferred_element_type=F32,
                       precision=lax.Precision.HIGHEST) + b_ref[0]


def _ada(c_pad, w_ada, b_ada):
    n = 3 * D_MODEL
    return pl.pallas_call(
        _ada_kernel,
        grid=(DEPTH, n // ADA_TN),
        in_specs=[
            pl.BlockSpec((ADA_ROWS, D_MODEL), lambda l, j: (0, 0)),
            pl.BlockSpec((1, D_MODEL, ADA_TN), lambda l, j: (l, 0, j)),
            pl.BlockSpec((1, 1, ADA_TN), lambda l, j: (l, 0, j)),
        ],
        out_specs=pl.BlockSpec((1, ADA_ROWS, ADA_TN), lambda l, j: (l, 0, j)),
        out_shape=jax.ShapeDtypeStruct((DEPTH, ADA_ROWS, n), F32),
        name="ada_mod",
    )(c_pad, w_ada, b_ada.reshape(DEPTH, 1, n))


def _inproj_kernel(x_ref, g_ref, scale_ref, shift_ref, w_ref, ret_ref, sb_ref, sg_ref):
    x = x_ref[0]
    inv = lax.rsqrt(jnp.mean(x * x, axis=-1, keepdims=True) + EPS)
    h = (x * inv * g_ref[...]) * (1.0 + scale_ref[0]) + shift_ref[0]
    hb = h.astype(BF16)
    n_ret = 4 * D_RET // PROJ_TN
    n_qkv = 3 * D_SB // PROJ_TN
    for n in range(n_ret):
        cols = slice(n * PROJ_TN, (n + 1) * PROJ_TN)
        ret_ref[0, :, cols] = jnp.dot(hb, w_ref[:, cols], preferred_element_type=F32)
    for n in range(n_qkv):
        wcols = slice((n_ret + n) * PROJ_TN, (n_ret + n + 1) * PROJ_TN)
        p = jnp.dot(hb, w_ref[:, wcols], preferred_element_type=F32)
        if n * PROJ_TN < D_SB:
            p = p * SB_Q_SCALE
        sb_ref[0, :, n * PROJ_TN:(n + 1) * PROJ_TN] = p.astype(BF16)
    wcols = slice((n_ret + n_qkv) * PROJ_TN, (n_ret + n_qkv + 1) * PROJ_TN)
    sg_ref[0] = jnp.dot(hb, w_ref[:, wcols], preferred_element_type=F32)


def _inproj(x, g, scale, shift, w_in_bf16):
    b, s, d = x.shape
    d_in = w_in_bf16.shape[1]
    return pl.pallas_call(
        _inproj_kernel,
        grid=(b, s // PROJ_TM),
        in_specs=[
            pl.BlockSpec((1, PROJ_TM, d), lambda bi, i: (bi, i, 0)),
            pl.BlockSpec((1, d), lambda bi, i: (0, 0)),
            pl.BlockSpec((1, 1, d), lambda bi, i: (bi, 0, 0)),
            pl.BlockSpec((1, 1, d), lambda bi, i: (bi, 0, 0)),
            pl.BlockSpec((d, d_in), lambda bi, i: (0, 0)),
        ],
        out_specs=[
            pl.BlockSpec((1, PROJ_TM, 4 * D_RET), lambda bi, i: (bi, i, 0)),
            pl.BlockSpec((1, PROJ_TM, 3 * D_SB), lambda bi, i: (bi, i, 0)),
            pl.BlockSpec((1, PROJ_TM, D_SB), lambda bi, i: (bi, i, 0)),
        ],
        out_shape=[
            jax.ShapeDtypeStruct((b, s, 4 * D_RET), F32),
            jax.ShapeDtypeStruct((b, s, 3 * D_SB), BF16),
            jax.ShapeDtypeStruct((b, s, D_SB), F32),
        ],
        compiler_params=pltpu.CompilerParams(vmem_limit_bytes=56 * MIB),
        name="in_proj",
    )(x, g.reshape(1, d), scale.reshape(b, 1, d), shift.reshape(b, 1, d), w_in_bf16)


def _retention_tables(seq):
    half = RET_HEAD_DIM // 2
    pos = np.arange(seq, dtype=np.float64)
    inv = ROPE_BASE ** (-np.arange(half, dtype=np.float64) / half)
    ang = pos[:, None] * inv[None, :]
    cos, sin = np.cos(ang), np.sin(ang)
    cos2 = np.concatenate([cos, cos], axis=-1)
    sin2 = np.concatenate([-sin, sin], axis=-1)

    log_gamma = np.log1p(-(2.0 ** (-5.0 - np.arange(RET_HEADS, dtype=np.float64))))
    idx = np.arange(RET_L, dtype=np.float64)
    diff = idx[:, None] - idx[None, :]
    same = (np.arange(RET_L)[:, None] // CHUNK) == (np.arange(RET_L)[None, :] // CHUNK)
    expo = np.where(same, np.abs(diff), diff)
    dmask = np.where(same | (diff > 0), np.exp(expo[None] * log_gamma[:, None, None]), 0.0)
    q_dec = np.exp((idx + 1.0)[None, :] * log_gamma[:, None])
    k_dec = np.exp((RET_L - 1.0 - idx)[None, :] * log_gamma[:, None])
    blk_dec = np.exp(RET_L * log_gamma)
    rep = lambda t: np.broadcast_to(t[:, :, None], (RET_HEADS, RET_L, LANES))
    blk = np.broadcast_to(blk_dec[:, None, None], (RET_HEADS, 1, LANES))
    return tuple(jnp.asarray(t, dtype=F32) for t in (cos2, sin2, dmask, rep(q_dec), rep(k_dec), blk))


def _retention_kernel(q_ref, k_ref, v_ref, g_ref, cos_ref, sin_ref, dm_ref, qd_ref, kd_ref,
                      bd_ref, o_ref, state_ref):
    @pl.when(pl.program_id(1) == 0)
    def _():
        state_ref[...] = jnp.zeros_like(state_ref)

    cos = cos_ref[...]
    sin = sin_ref[...]
    half = RET_HEAD_DIM // 2
    for h in range(RET_HEADS):
        cols = slice(h * RET_HEAD_DIM, (h + 1) * RET_HEAD_DIM)
        q = q_ref[0, :, cols]
        k = k_ref[0, :, cols]
        q = q * cos + pltpu.roll(q, half, axis=1) * sin
        k = (k * cos + pltpu.roll(k, half, axis=1) * sin) * (RET_HEAD_DIM ** -0.5)
        vb = v_ref[0, :, cols].astype(BF16)
        qb = q.astype(BF16)
        scores = lax.dot_general(qb, k.astype(BF16), (((1,), (1,)), ((), ())),
                                 preferred_element_type=F32) * dm_ref[h]
        o = jnp.dot(scores.astype(BF16), vb, preferred_element_type=F32)
        state = state_ref[h]
        o = o + jnp.dot(qb, state.astype(BF16), preferred_element_type=F32) * qd_ref[h]
        kv = lax.dot_general((k * kd_ref[h]).astype(BF16), vb, (((0,), (0,)), ((), ())),
                             preferred_element_type=F32)
        state_ref[h] = state * bd_ref[h] + kv
        mu = jnp.mean(o, axis=-1, keepdims=True)
        oc = o - mu
        var = jnp.mean(oc * oc, axis=-1, keepdims=True)
        y = oc * lax.rsqrt(var + EPS) * _silu(g_ref[0, :, cols])
        o_ref[0, :, cols] = y.astype(o_ref.dtype)


def _retention(ret, tables):
    b, s, _ = ret.shape
    cos2, sin2, dmask, q_dec, k_dec, blk = tables
    col = lambda c: pl.BlockSpec((1, RET_L, D_RET), lambda bi, i: (bi, i, c))
    whole = lambda a: pl.BlockSpec(a.shape, lambda bi, i: (0,) * a.ndim)
    return pl.pallas_call(
        _retention_kernel,
        grid=(b, s // RET_L),
        in_specs=[
            col(0), col(1), col(2), col(3),
            pl.BlockSpec((RET_L, LANES), lambda bi, i: (i, 0)),
            pl.BlockSpec((RET_L, LANES), lambda bi, i: (i, 0)),
            whole(dmask), whole(q_dec), whole(k_dec), whole(blk),
        ],
        out_specs=pl.BlockSpec((1, RET_L, D_RET), lambda bi, i: (bi, i, 0)),
        out_shape=jax.ShapeDtypeStruct((b, s, D_RET), BF16),
        scratch_shapes=[pltpu.VMEM((RET_HEADS, RET_HEAD_DIM, RET_HEAD_DIM), F32)],
        name="retention",
    )(ret, ret, ret, ret, cos2, sin2, dmask, q_dec, k_dec, blk)


def _softplus2(u):
    ln = jnp.log(1.0 + jnp.exp2(jnp.minimum(u, SB_EXP2_CLAMP)))
    return jnp.maximum(u, ln * LOG2E_HI + ln * LOG2E_LO)


def _sb_scores(qm_ref, k_ref, u_ref, sp_ref, slot, key0, diagonal):
    for pair in range(SB_PAIRS):
        kb = k_ref[0, pl.ds(key0, SB_TK), pair * LANES:(pair + 1) * LANES]
        qm = qm_ref[pair].reshape(2 * SB_TQ, LANES)
        u = lax.dot_general(qm, kb, (((1,), (1,)), ((), ())), preferred_element_type=F32)
        sp = _softplus2(u.astype(BF16))
        if diagonal:
            t_loc = lax.broadcasted_iota(jnp.int32, (SB_TQ, SB_TK), 0)
            s_loc = lax.broadcasted_iota(jnp.int32, (SB_TQ, SB_TK), 1)
            m = s_loc < t_loc
            mask = jnp.concatenate([m, m], axis=0)
            sp = jnp.where(mask, sp, jnp.zeros_like(sp))
            u = jnp.where(mask, u, SB_MASKED)
        u_ref[pair, slot] = u.reshape(2, SB_TQ, SB_TK)
        sp_ref[pair, slot] = sp.reshape(2, SB_TQ, SB_TK)


def _sb_values(v_ref, tri_ref, u_ref, sp_ref, acc_ref, ncarry_ref, slot, key0):
    for pair in range(SB_PAIRS):
        vb = v_ref[0, pl.ds(key0, SB_TK), pair * LANES:(pair + 1) * LANES]
        sp = sp_ref[pair, slot].reshape(2 * SB_TQ, SB_TK)
        r = jnp.dot(sp, tri_ref[...], preferred_element_type=F32)
        u = u_ref[pair, slot].reshape(2 * SB_TQ, SB_TK)
        ncarry = ncarry_ref[pair].reshape(2 * SB_TQ, LANES)
        a = jnp.exp2(u - r + jnp.concatenate([ncarry] * (SB_TK // LANES), axis=1))
        pv = jnp.dot(a.astype(BF16), vb, preferred_element_type=F32)
        acc_ref[pair] += pv.reshape(2, SB_TQ, LANES)
        ncarry_ref[pair] = (ncarry - r[:, :1]).reshape(2, SB_TQ, LANES)


def _sb_kernel(q_ref, k_ref, v_ref, g_ref, tri_ref, o_ref,
               qm_ref, acc_ref, ncarry_ref, u_ref, sp_ref):
    qi = pl.program_id(1)
    lane = lax.broadcasted_iota(jnp.int32, (SB_TQ, LANES), 1)
    first = lane < SB_HEAD_DIM
    for pair in range(SB_PAIRS):
        q = q_ref[0, :, pair * LANES:(pair + 1) * LANES]
        zero = jnp.zeros_like(q)
        qm_ref[pair, 0] = jnp.where(first, q, zero)
        qm_ref[pair, 1] = jnp.where(first, zero, q)
    acc_ref[...] = jnp.zeros_like(acc_ref)
    ncarry_ref[...] = jnp.zeros_like(ncarry_ref)

    scores = functools.partial(_sb_scores, qm_ref, k_ref, u_ref, sp_ref)
    values = functools.partial(_sb_values, v_ref, tri_ref, u_ref, sp_ref, acc_ref, ncarry_ref)
    key_of = lambda blk: pl.multiple_of(blk * SB_TK, SB_TK)

    def live():
        return (jnp.max(ncarry_ref[...]) > -SB_DEAD).astype(jnp.int32)

    scores(0, key_of(qi), True)

    @pl.when(qi == 0)
    def _():
        values(0, key_of(qi))

    @pl.when(qi > 0)
    def _():
        scores(1, key_of(qi - 1), False)
        values(0, key_of(qi))
        values(1, key_of(qi - 1))

        def body(state):
            blk, _ = state
            scores(0, key_of(blk), False)
            values(0, key_of(blk))
            return blk - 1, live()

        lax.while_loop(lambda state: jnp.logical_and(state[0] >= 0, state[1] > 0),
                       body, (qi - 2, live()))

    for pair in range(SB_PAIRS):
        lanes = slice(pair * LANES, (pair + 1) * LANES)
        y = jnp.where(first, acc_ref[pair, 0], acc_ref[pair, 1]) * _silu(g_ref[0, :, lanes])
        o_ref[0, :, lanes] = y.astype(o_ref.dtype)


def _stick_breaking(sb, sg, tri):
    b, s, _ = sb.shape
    resident = lambda col: pl.BlockSpec((1, s, D_SB), lambda bi, i: (bi, 0, col),
                                        pipeline_mode=pl.Buffered(1))
    return pl.pallas_call(
        _sb_kernel,
        grid=(b, s // SB_TQ),
        in_specs=[
            pl.BlockSpec((1, SB_TQ, D_SB), lambda bi, i: (bi, i, 0)),
            resident(1),
            resident(2),
            pl.BlockSpec((1, SB_TQ, D_SB), lambda bi, i: (bi, i, 0)),
            pl.BlockSpec(tri.shape, lambda bi, i: (0, 0)),
        ],
        out_specs=pl.BlockSpec((1, SB_TQ, D_SB), lambda bi, i: (bi, i, 0)),
        out_shape=jax.ShapeDtypeStruct((b, s, D_SB), BF16),
        scratch_shapes=[
            pltpu.VMEM((SB_PAIRS, 2, SB_TQ, LANES), BF16),
            pltpu.VMEM((SB_PAIRS, 2, SB_TQ, LANES), F32),
            pltpu.VMEM((SB_PAIRS, 2, SB_TQ, LANES), F32),
            pltpu.VMEM((SB_PAIRS, 2, 2, SB_TQ, SB_TK), F32),
            pltpu.VMEM((SB_PAIRS, 2, 2, SB_TQ, SB_TK), BF16),
        ],
        compiler_params=pltpu.CompilerParams(vmem_limit_bytes=48 * MIB),
        name="stick_breaking",
    )(sb, sb, sb, sg, tri)


def _outproj_kernel(yr_ref, ys_ref, w_ref, x_ref, gate_ref, fg_ref, o_ref, *, final):
    y = jnp.dot(yr_ref[0], w_ref[:D_RET, :], preferred_element_type=F32)
    y = y + jnp.dot(ys_ref[0], w_ref[D_RET:, :], preferred_element_type=F32)
    xn = x_ref[0] + gate_ref[0] * y
    if final:
        inv = lax.rsqrt(jnp.mean(xn * xn, axis=-1, keepdims=True) + EPS)
        xn = xn * inv * fg_ref[...]
    o_ref[0] = xn.astype(o_ref.dtype)


def _outproj(y_ret, y_sb, w_out_bf16, x, gate, final_g, final):
    b, s, d = x.shape
    return pl.pallas_call(
        functools.partial(_outproj_kernel, final=final),
        grid=(b, s // OUT_TM),
        in_specs=[
            pl.BlockSpec((1, OUT_TM, D_RET), lambda bi, i: (bi, i, 0)),
            pl.BlockSpec((1, OUT_TM, D_SB), lambda bi, i: (bi, i, 0)),
            pl.BlockSpec(w_out_bf16.shape, lambda bi, i: (0, 0)),
            pl.BlockSpec((1, OUT_TM, d), lambda bi, i: (bi, i, 0)),
            pl.BlockSpec((1, 1, d), lambda bi, i: (bi, 0, 0)),
            pl.BlockSpec((1, d), lambda bi, i: (0, 0)),
        ],
        out_specs=pl.BlockSpec((1, OUT_TM, d), lambda bi, i: (bi, i, 0)),
        out_shape=jax.ShapeDtypeStruct((b, s, d), x.dtype),
        name="out_proj_final" if final else "out_proj",
    )(y_ret, y_sb, w_out_bf16, x, gate.reshape(b, 1, d), final_g.reshape(1, d))


def kernel(x, c, norm_g, w_ada, b_ada, w_in, w_out, final_g):
    b, s, d = x.shape
    assert d == D_MODEL and s % max(SB_TQ, RET_L, PROJ_TM, OUT_TM) == 0 and SB_TQ == SB_TK
    c_pad = jnp.zeros((ADA_ROWS, d), F32).at[:b].set(c.astype(F32))
    mod = _ada(c_pad, w_ada, b_ada)[:, :b]
    tables = _retention_tables(s)
    tri = jnp.asarray(np.tril(np.ones((SB_TK, SB_TK))), dtype=BF16)
    w_in_b = w_in.astype(BF16)
    w_out_b = w_out.astype(BF16)

    h = x.astype(F32)
    for layer in range(DEPTH):
        shift, scale, gate = jnp.split(mod[layer], 3, axis=-1)
        ret, sb, sg = _inproj(h, norm_g[layer], scale, shift, w_in_b[layer])
        y_ret = _retention(ret, tables)
        y_sb = _stick_breaking(sb, sg, tri)
        h = _outproj(y_ret, y_sb, w_out_b[layer], h, gate, final_g, layer == DEPTH - 1)
    return h.astype(x.dtype)
```

```python
import functools

import jax
import jax.numpy as jnp
import numpy as np
from jax import lax
from jax.experimental import pallas as pl
from jax.experimental.pallas import tpu as pltpu

D_MODEL = 1024
DEPTH = 2
CHUNK = 64
D_RET = 512
D_SB = 512
RET_HEADS = 4
RET_HEAD_DIM = 128
SB_HEADS = 8
SB_HEAD_DIM = 64
ROPE_BASE = 10000.0
EPS = 1e-6
LOG2E = 1.4426950408889634
LOG2E_HI = 1.4453125
LOG2E_LO = LOG2E - LOG2E_HI

LANES = 128
SB_PAIRS = SB_HEADS * SB_HEAD_DIM // LANES
MIB = 1024 * 1024

ADA_ROWS = 8
ADA_TN = 768
PROJ_TM = 512
PROJ_TN = 512
RET_L = 512
SB_TK = 256
SB_SUB = 2
SB_TQ = SB_SUB * SB_TK
OUT_TM = 1024

SB_EXP2_CLAMP = 126.0
SB_DEAD = 151.0
SB_MASKED = -1e30
SB_Q_SCALE = LOG2E * SB_HEAD_DIM ** -0.5

F32 = jnp.float32
BF16 = jnp.bfloat16


def _silu(x):
    return x * (1.0 / (1.0 + jnp.exp(-x)))


def _ada_kernel(c_ref, w_ref, b_ref, o_ref):
    c_act = _silu(c_ref[...])
    o_ref[0] = jnp.dot(c_act, w_ref[0], preferred_element_type=F32,
                       precision=lax.Precision.HIGHEST) + b_ref[0]


def _ada(c_pad, w_ada, b_ada):
    n = 3 * D_MODEL
    return pl.pallas_call(
        _ada_kernel,
        grid=(DEPTH, n // ADA_TN),
        in_specs=[
            pl.BlockSpec((ADA_ROWS, D_MODEL), lambda l, j: (0, 0)),
            pl.BlockSpec((1, D_MODEL, ADA_TN), lambda l, j: (l, 0, j)),
            pl.BlockSpec((1, 1, ADA_TN), lambda l, j: (l, 0, j)),
        ],
        out_specs=pl.BlockSpec((1, ADA_ROWS, ADA_TN), lambda l, j: (l, 0, j)),
        out_shape=jax.ShapeDtypeStruct((DEPTH, ADA_ROWS, n), F32),
        name="ada_mod",
    )(c_pad, w_ada, b_ada.reshape(DEPTH, 1, n))


def _inproj_kernel(x_ref, g_ref, scale_ref, shift_ref, w_ref, ret_ref, sb_ref, sg_ref):
    x = x_ref[0]
    inv = lax.rsqrt(jnp.mean(x * x, axis=-1, keepdims=True) + EPS)
    h = (x * inv * g_ref[...]) * (1.0 + scale_ref[0]) + shift_ref[0]
    hb = h.astype(BF16)
    n_ret = 4 * D_RET // PROJ_TN
    n_qkv = 3 * D_SB // PROJ_TN
    for n in range(n_ret):
        cols = slice(n * PROJ_TN, (n + 1) * PROJ_TN)
        ret_ref[0, :, cols] = jnp.dot(hb, w_ref[:, cols], preferred_element_type=F32)
    for n in range(n_qkv):
        wcols = slice((n_ret + n) * PROJ_TN, (n_ret + n + 1) * PROJ_TN)
        p = jnp.dot(hb, w_ref[:, wcols], preferred_element_type=F32)
        if n * PROJ_TN < D_SB:
            p = p * SB_Q_SCALE
        sb_ref[0, :, n * PROJ_TN:(n + 1) * PROJ_TN] = p.astype(BF16)
    wcols = slice((n_ret + n_qkv) * PROJ_TN, (n_ret + n_qkv + 1) * PROJ_TN)
    sg_ref[0] = jnp.dot(hb, w_ref[:, wcols], preferred_element_type=F32)


def _inproj(x, g, scale, shift, w_in_bf16):
    b, s, d = x.shape
    d_in = w_in_bf16.shape[1]
    return pl.pallas_call(
        _inproj_kernel,
        grid=(b, s // PROJ_TM),
        in_specs=[
            pl.BlockSpec((1, PROJ_TM, d), lambda bi, i: (bi, i, 0)),
            pl.BlockSpec((1, d), lambda bi, i: (0, 0)),
            pl.BlockSpec((1, 1, d), lambda bi, i: (bi, 0, 0)),
            pl.BlockSpec((1, 1, d), lambda bi, i: (bi, 0, 0)),
            pl.BlockSpec((d, d_in), lambda bi, i: (0, 0)),
        ],
        out_specs=[
            pl.BlockSpec((1, PROJ_TM, 4 * D_RET), lambda bi, i: (bi, i, 0)),
            pl.BlockSpec((1, PROJ_TM, 3 * D_SB), lambda bi, i: (bi, i, 0)),
            pl.BlockSpec((1, PROJ_TM, D_SB), lambda bi, i: (bi, i, 0)),
        ],
        out_shape=[
            jax.ShapeDtypeStruct((b, s, 4 * D_RET), F32),
            jax.ShapeDtypeStruct((b, s, 3 * D_SB), BF16),
            jax.ShapeDtypeStruct((b, s, D_SB), F32),
        ],
        compiler_params=pltpu.CompilerParams(vmem_limit_bytes=56 * MIB),
        name="in_proj",
    )(x, g.reshape(1, d), scale.reshape(b, 1, d), shift.reshape(b, 1, d), w_in_bf16)


def _retention_tables(seq):
    half = RET_HEAD_DIM // 2
    pos = np.arange(seq, dtype=np.float64)
    inv = ROPE_BASE ** (-np.arange(half, dtype=np.float64) / half)
    ang = pos[:, None] * inv[None, :]
    cos, sin = np.cos(ang), np.sin(ang)
    cos2 = np.concatenate([cos, cos], axis=-1)
    sin2 = np.concatenate([-sin, sin], axis=-1)

    log_gamma = np.log1p(-(2.0 ** (-5.0 - np.arange(RET_HEADS, dtype=np.float64))))
    idx = np.arange(RET_L, dtype=np.float64)
    diff = idx[:, None] - idx[None, :]
    same = (np.arange(RET_L)[:, None] // CHUNK) == (np.arange(RET_L)[None, :] // CHUNK)
    expo = np.where(same, np.abs(diff), diff)
    dmask = np.where(same | (diff > 0), np.exp(expo[None] * log_gamma[:, None, None]), 0.0)
    q_dec = np.exp((idx + 1.0)[None, :] * log_gamma[:, None])
    k_dec = np.exp((RET_L - 1.0 - idx)[None, :] * log_gamma[:, None])
    blk_dec = np.exp(RET_L * log_gamma)
    rep = lambda t: np.broadcast_to(t[:, :, None], (RET_HEADS, RET_L, LANES))
    blk = np.broadcast_to(blk_dec[:, None, None], (RET_HEADS, 1, LANES))
    return tuple(jnp.asarray(t, dtype=F32) for t in (cos2, sin2, dmask, rep(q_dec), rep(k_dec), blk))


def _retention_kernel(q_ref, k_ref, v_ref, g_ref, cos_ref, sin_ref, dm_ref, qd_ref, kd_ref,
                      bd_ref, o_ref, state_ref):
    @pl.when(pl.program_id(1) == 0)
    def _():
        state_ref[...] = jnp.zeros_like(state_ref)

    cos = cos_ref[...]
    sin = sin_ref[...]
    half = RET_HEAD_DIM // 2
    for h in range(RET_HEADS):
        cols = slice(h * RET_HEAD_DIM, (h + 1) * RET_HEAD_DIM)
        q = q_ref[0, :, cols]
        k = k_ref[0, :, cols]
        q = q * cos + pltpu.roll(q, half, axis=1) * sin
        k = (k * cos + pltpu.roll(k, half, axis=1) * sin) * (RET_HEAD_DIM ** -0.5)
        vb = v_ref[0, :, cols].astype(BF16)
        qb = q.astype(BF16)
        scores = lax.dot_general(qb, k.astype(BF16), (((1,), (1,)), ((), ())),
                                 preferred_element_type=F32) * dm_ref[h]
        o = jnp.dot(scores.astype(BF16), vb, preferred_element_type=F32)
        state = state_ref[h]
        o = o + jnp.dot(qb, state.astype(BF16), preferred_element_type=F32) * qd_ref[h]
        kv = lax.dot_general((k * kd_ref[h]).astype(BF16), vb, (((0,), (0,)), ((), ())),
                             preferred_element_type=F32)
        state_ref[h] = state * bd_ref[h] + kv
        mu = jnp.mean(o, axis=-1, keepdims=True)
        oc = o - mu
        var = jnp.mean(oc * oc, axis=-1, keepdims=True)
        y = oc * lax.rsqrt(var + EPS) * _silu(g_ref[0, :, cols])
        o_ref[0, :, cols] = y.astype(o_ref.dtype)


def _retention(ret, tables):
    b, s, _ = ret.shape
    cos2, sin2, dmask, q_dec, k_dec, blk = tables
    col = lambda c: pl.BlockSpec((1, RET_L, D_RET), lambda bi, i: (bi, i, c))
    whole = lambda a: pl.BlockSpec(a.shape, lambda bi, i: (0,) * a.ndim)
    return pl.pallas_call(
        _retention_kernel,
        grid=(b, s // RET_L),
        in_specs=[
            col(0), col(1), col(2), col(3),
            pl.BlockSpec((RET_L, LANES), lambda bi, i: (i, 0)),
            pl.BlockSpec((RET_L, LANES), lambda bi, i: (i, 0)),
            whole(dmask), whole(q_dec), whole(k_dec), whole(blk),
        ],
        out_specs=pl.BlockSpec((1, RET_L, D_RET), lambda bi, i: (bi, i, 0)),
        out_shape=jax.ShapeDtypeStruct((b, s, D_RET), BF16),
        scratch_shapes=[pltpu.VMEM((RET_HEADS, RET_HEAD_DIM, RET_HEAD_DIM), F32)],
        name="retention",
    )(ret, ret, ret, ret, cos2, sin2, dmask, q_dec, k_dec, blk)


def _softplus2(u):
    ln = jnp.log(1.0 + jnp.exp2(jnp.minimum(u, SB_EXP2_CLAMP)))
    return jnp.maximum(u, ln * LOG2E_HI + ln * LOG2E_LO)


def _sb_scores(qm_ref, k_ref, u_ref, sp_ref, sub, slot, key0, diagonal, bias=None):
    for pair in range(SB_PAIRS):
        kb = k_ref[0, pl.ds(key0, SB_TK), pair * LANES:(pair + 1) * LANES]
        qm = qm_ref[sub, pair].reshape(2 * SB_TK, LANES)
        u = lax.dot_general(qm, kb, (((1,), (1,)), ((), ())), preferred_element_type=F32)
        if bias is not None:
            u = u + bias
        sp = _softplus2(u.astype(BF16))
        if diagonal:
            t_loc = lax.broadcasted_iota(jnp.int32, (SB_TK, SB_TK), 0)
            s_loc = lax.broadcasted_iota(jnp.int32, (SB_TK, SB_TK), 1)
            m = s_loc < t_loc
            mask = jnp.concatenate([m, m], axis=0)
            sp = jnp.where(mask, sp, jnp.zeros_like(sp))
            u = jnp.where(mask, u, SB_MASKED)
        u_ref[slot, pair] = u.reshape(2, SB_TK, SB_TK)
        sp_ref[slot, pair] = sp.reshape(2, SB_TK, SB_TK)


def _sb_values(v_ref, tri_ref, u_ref, sp_ref, acc_ref, ncarry_ref, sub, slot, key0):
    for pair in range(SB_PAIRS):
        vb = v_ref[0, pl.ds(key0, SB_TK), pair * LANES:(pair + 1) * LANES]
        sp = sp_ref[slot, pair].reshape(2 * SB_TK, SB_TK)
        r = jnp.dot(sp, tri_ref[...], preferred_element_type=F32)
        u = u_ref[slot, pair].reshape(2 * SB_TK, SB_TK)
        ncarry = ncarry_ref[sub, pair].reshape(2 * SB_TK, LANES)
        a = jnp.exp2(u - r + jnp.concatenate([ncarry] * (SB_TK // LANES), axis=1))
        pv = jnp.dot(a.astype(BF16), vb, preferred_element_type=F32)
        acc_ref[sub, pair] += pv.reshape(2, SB_TK, LANES)
        ncarry_ref[sub, pair] = (ncarry - r[:, :1]).reshape(2, SB_TK, LANES)


def _sb_kernel(q_ref, k_ref, v_ref, g_ref, tri_ref, o_ref,
               qm_ref, acc_ref, ncarry_ref, u_ref, sp_ref):
    step = pl.program_id(1)
    lane = lax.broadcasted_iota(jnp.int32, (SB_TK, LANES), 1)
    first = lane < SB_HEAD_DIM
    for sub in range(SB_SUB):
        for pair in range(SB_PAIRS):
            q = q_ref[0, sub * SB_TK:(sub + 1) * SB_TK, pair * LANES:(pair + 1) * LANES]
            zero = jnp.zeros_like(q)
            qm_ref[sub, pair, 0] = jnp.where(first, q, zero)
            qm_ref[sub, pair, 1] = jnp.where(first, zero, q)
    acc_ref[...] = jnp.zeros_like(acc_ref)
    ncarry_ref[...] = jnp.zeros_like(ncarry_ref)

    scores = functools.partial(_sb_scores, qm_ref, k_ref, u_ref, sp_ref)
    values = functools.partial(_sb_values, v_ref, tri_ref, u_ref, sp_ref, acc_ref, ncarry_ref)
    key_of = lambda blk: pl.multiple_of(blk * SB_TK, SB_TK)

    own = lambda sub: step * SB_SUB + sub
    no_prev = jnp.where(step == 0, SB_MASKED, 0.0)
    sequence = []
    for sub in range(SB_SUB):
        sequence.append((sub, own(sub), True, None))
        prev = jnp.maximum(own(sub) - 1, 0) if sub == 0 else own(sub) - 1
        sequence.append((sub, prev, False, no_prev if sub == 0 else None))
    for n, (sub, blk, diagonal, bias) in enumerate(sequence):
        scores(sub, n % 2, key_of(blk), diagonal, bias)
        if n > 0:
            p_sub, p_blk, _, _ = sequence[n - 1]
            values(p_sub, (n - 1) % 2, key_of(p_blk))
    p_sub, p_blk, _, _ = sequence[-1]
    values(p_sub, (len(sequence) - 1) % 2, key_of(p_blk))

    live = lambda sub: (jnp.max(ncarry_ref[sub]) > -SB_DEAD).astype(jnp.int32)
    live_now = [live(sub) for sub in range(SB_SUB)]
    for sub in range(SB_SUB):
        def body(state):
            blk, _ = state
            scores(sub, 0, key_of(blk), False)
            values(sub, 0, key_of(blk))
            return blk - 1, live(sub)

        lax.while_loop(lambda state: jnp.logical_and(state[0] >= 0, state[1] > 0),
                       body, (own(sub) - 2, live_now[sub]))

    for sub in range(SB_SUB):
        rows = slice(sub * SB_TK, (sub + 1) * SB_TK)
        for pair in range(SB_PAIRS):
            lanes = slice(pair * LANES, (pair + 1) * LANES)
            y = jnp.where(first, acc_ref[sub, pair, 0], acc_ref[sub, pair, 1])
            o_ref[0, rows, lanes] = (y * _silu(g_ref[0, rows, lanes])).astype(o_ref.dtype)


def _stick_breaking(sb, sg, tri):
    b, s, _ = sb.shape
    resident = lambda col: pl.BlockSpec((1, s, D_SB), lambda bi, i: (bi, 0, col))
    return pl.pallas_call(
        _sb_kernel,
        grid=(b, s // SB_TQ),
        in_specs=[
            pl.BlockSpec((1, SB_TQ, D_SB), lambda bi, i: (bi, i, 0)),
            resident(1),
            resident(2),
            pl.BlockSpec((1, SB_TQ, D_SB), lambda bi, i: (bi, i, 0)),
            pl.BlockSpec(tri.shape, lambda bi, i: (0, 0)),
        ],
        out_specs=pl.BlockSpec((1, SB_TQ, D_SB), lambda bi, i: (bi, i, 0)),
        out_shape=jax.ShapeDtypeStruct((b, s, D_SB), BF16),
        scratch_shapes=[
            pltpu.VMEM((SB_SUB, SB_PAIRS, 2, SB_TK, LANES), BF16),
            pltpu.VMEM((SB_SUB, SB_PAIRS, 2, SB_TK, LANES), F32),
            pltpu.VMEM((SB_SUB, SB_PAIRS, 2, SB_TK, LANES), F32),
            pltpu.VMEM((2, SB_PAIRS, 2, SB_TK, SB_TK), F32),
            pltpu.VMEM((2, SB_PAIRS, 2, SB_TK, SB_TK), BF16),
        ],
        compiler_params=pltpu.CompilerParams(vmem_limit_bytes=56 * MIB),
        name="stick_breaking",
    )(sb, sb, sb, sg, tri)


def _outproj_kernel(yr_ref, ys_ref, w_ref, x_ref, gate_ref, fg_ref, o_ref, *, final):
    y = jnp.dot(yr_ref[0], w_ref[:D_RET, :], preferred_element_type=F32)
    y = y + jnp.dot(ys_ref[0], w_ref[D_RET:, :], preferred_element_type=F32)
    xn = x_ref[0] + gate_ref[0] * y
    if final:
        inv = lax.rsqrt(jnp.mean(xn * xn, axis=-1, keepdims=True) + EPS)
        xn = xn * inv * fg_ref[...]
    o_ref[0] = xn.astype(o_ref.dtype)


def _outproj(y_ret, y_sb, w_out_bf16, x, gate, final_g, final):
    b, s, d = x.shape
    return pl.pallas_call(
        functools.partial(_outproj_kernel, final=final),
        grid=(b, s // OUT_TM),
        in_specs=[
            pl.BlockSpec((1, OUT_TM, D_RET), lambda bi, i: (bi, i, 0)),
            pl.BlockSpec((1, OUT_TM, D_SB), lambda bi, i: (bi, i, 0)),
            pl.BlockSpec(w_out_bf16.shape, lambda bi, i: (0, 0)),
            pl.BlockSpec((1, OUT_TM, d), lambda bi, i: (bi, i, 0)),
            pl.BlockSpec((1, 1, d), lambda bi, i: (bi, 0, 0)),
            pl.BlockSpec((1, d), lambda bi, i: (0, 0)),
        ],
        out_specs=pl.BlockSpec((1, OUT_TM, d), lambda bi, i: (bi, i, 0)),
        out_shape=jax.ShapeDtypeStruct((b, s, d), x.dtype),
        compiler_params=pltpu.CompilerParams(vmem_limit_bytes=40 * MIB),
        name="out_proj_final" if final else "out_proj",
    )(y_ret, y_sb, w_out_bf16, x, gate.reshape(b, 1, d), final_g.reshape(1, d))


def kernel(x, c, norm_g, w_ada, b_ada, w_in, w_out, final_g):
    b, s, d = x.shape
    assert d == D_MODEL and s % max(SB_TQ, RET_L, PROJ_TM, OUT_TM) == 0
    c_pad = jnp.zeros((ADA_ROWS, d), F32).at[:b].set(c.astype(F32))
    mod = _ada(c_pad, w_ada, b_ada)[:, :b]
    tables = _retention_tables(s)
    tri = jnp.asarray(np.tril(np.ones((SB_TK, SB_TK))), dtype=BF16)
    w_in_b = w_in.astype(BF16)
    w_out_b = w_out.astype(BF16)

    h = x.astype(F32)
    for layer in range(DEPTH):
        shift, scale, gate = jnp.split(mod[layer], 3, axis=-1)
        ret, sb, sg = _inproj(h, norm_g[layer], scale, shift, w_in_b[layer])
        y_ret = _retention(ret, tables)
        y_sb = _stick_breaking(sb, sg, tri)
        h = _outproj(y_ret, y_sb, w_out_b[layer], h, gate, final_g, layer == DEPTH - 1)
    return h.astype(x.dtype)
```

```python
import functools

import jax
import jax.numpy as jnp
import numpy as np
from jax import lax
from jax.experimental import pallas as pl
from jax.experimental.pallas import tpu as pltpu

D_MODEL = 1024
DEPTH = 2
CHUNK = 64
D_RET = 512
D_SB = 512
RET_HEADS = 4
RET_HEAD_DIM = 128
SB_HEADS = 8
SB_HEAD_DIM = 64
ROPE_BASE = 10000.0
EPS = 1e-6
LOG2E = 1.4426950408889634
LOG2E_HI = 1.4453125
LOG2E_LO = LOG2E - LOG2E_HI

LANES = 128
SB_PAIRS = SB_HEADS * SB_HEAD_DIM // LANES
MIB = 1024 * 1024

ADA_ROWS = 8
ADA_TN = 768
PROJ_TN = 512
RET_L = 512
PROJ_TM = RET_L
SB_TK = 256
SB_SUB = 2
SB_TQ = SB_SUB * SB_TK
OUT_TM = 1024

SB_EXP2_CLAMP = 126.0
SB_DEAD = 151.0
SB_MASKED = -1e30
SB_Q_SCALE = LOG2E * SB_HEAD_DIM ** -0.5

F32 = jnp.float32
BF16 = jnp.bfloat16


def _silu(x):
    return x * (1.0 / (1.0 + jnp.exp(-x)))


def _ada_kernel(c_ref, w_ref, b_ref, o_ref):
    c_act = _silu(c_ref[...])
    o_ref[0] = jnp.dot(c_act, w_ref[0], preferred_element_type=F32,
                       precision=lax.Precision.HIGHEST) + b_ref[0]


def _ada(c_pad, w_ada, b_ada):
    n = 3 * D_MODEL
    return pl.pallas_call(
        _ada_kernel,
        grid=(DEPTH, n // ADA_TN),
        in_specs=[
            pl.BlockSpec((ADA_ROWS, D_MODEL), lambda l, j: (0, 0)),
            pl.BlockSpec((1, D_MODEL, ADA_TN), lambda l, j: (l, 0, j)),
            pl.BlockSpec((1, 1, ADA_TN), lambda l, j: (l, 0, j)),
        ],
        out_specs=pl.BlockSpec((1, ADA_ROWS, ADA_TN), lambda l, j: (l, 0, j)),
        out_shape=jax.ShapeDtypeStruct((DEPTH, ADA_ROWS, n), F32),
        name="ada_mod",
    )(c_pad, w_ada, b_ada.reshape(DEPTH, 1, n))


def _retention_tables(seq):
    half = RET_HEAD_DIM // 2
    pos = np.arange(seq, dtype=np.float64)
    inv = ROPE_BASE ** (-np.arange(half, dtype=np.float64) / half)
    ang = pos[:, None] * inv[None, :]
    cos, sin = np.cos(ang), np.sin(ang)
    cos2 = np.concatenate([cos, cos], axis=-1)
    sin2 = np.concatenate([-sin, sin], axis=-1)

    log_gamma = np.log1p(-(2.0 ** (-5.0 - np.arange(RET_HEADS, dtype=np.float64))))
    idx = np.arange(RET_L, dtype=np.float64)
    diff = idx[:, None] - idx[None, :]
    same = (np.arange(RET_L)[:, None] // CHUNK) == (np.arange(RET_L)[None, :] // CHUNK)
    expo = np.where(same, np.abs(diff), diff)
    dmask = np.where(same | (diff > 0), np.exp(expo[None] * log_gamma[:, None, None]), 0.0)
    q_dec = np.exp((idx + 1.0)[None, :] * log_gamma[:, None])
    k_dec = np.exp((RET_L - 1.0 - idx)[None, :] * log_gamma[:, None])
    blk_dec = np.exp(RET_L * log_gamma)
    rep = lambda t: np.broadcast_to(t[:, :, None], (RET_HEADS, RET_L, LANES))
    blk = np.broadcast_to(blk_dec[:, None, None], (RET_HEADS, 1, LANES))
    return tuple(jnp.asarray(t, dtype=F32) for t in (cos2, sin2, dmask, rep(q_dec), rep(k_dec), blk))


def _retention_head(h, rq, rk, rv, rg, cos, sin, dm_ref, qd_ref, kd_ref, bd_ref, state_ref):
    half = RET_HEAD_DIM // 2
    cols = slice(h * RET_HEAD_DIM, (h + 1) * RET_HEAD_DIM)
    q = rq[:, cols]
    k = rk[:, cols]
    q = q * cos + pltpu.roll(q, half, axis=1) * sin
    k = (k * cos + pltpu.roll(k, half, axis=1) * sin) * (RET_HEAD_DIM ** -0.5)
    vb = rv[:, cols].astype(BF16)
    qb = q.astype(BF16)
    scores = lax.dot_general(qb, k.astype(BF16), (((1,), (1,)), ((), ())),
                             preferred_element_type=F32) * dm_ref[h]
    o = jnp.dot(scores.astype(BF16), vb, preferred_element_type=F32)
    state = state_ref[h]
    o = o + jnp.dot(qb, state.astype(BF16), preferred_element_type=F32) * qd_ref[h]
    kv = lax.dot_general((k * kd_ref[h]).astype(BF16), vb, (((0,), (0,)), ((), ())),
                         preferred_element_type=F32)
    state_ref[h] = state * bd_ref[h] + kv
    mu = jnp.mean(o, axis=-1, keepdims=True)
    oc = o - mu
    var = jnp.mean(oc * oc, axis=-1, keepdims=True)
    return oc * lax.rsqrt(var + EPS) * _silu(rg[:, cols])


def _inproj_kernel(x_ref, g_ref, scale_ref, shift_ref, w_ref, cos_ref, sin_ref, dm_ref, qd_ref,
                   kd_ref, bd_ref, yret_ref, sb_ref, sg_ref, state_ref):
    @pl.when(pl.program_id(1) == 0)
    def _():
        state_ref[...] = jnp.zeros_like(state_ref)

    x = x_ref[0]
    inv = lax.rsqrt(jnp.mean(x * x, axis=-1, keepdims=True) + EPS)
    h = (x * inv * g_ref[...]) * (1.0 + scale_ref[0]) + shift_ref[0]
    hb = h.astype(BF16)
    proj = lambda n: jnp.dot(hb, w_ref[:, n * PROJ_TN:(n + 1) * PROJ_TN], preferred_element_type=F32)
    n_ret = 4 * D_RET // PROJ_TN
    n_qkv = 3 * D_SB // PROJ_TN
    rq, rk, rv, rg = [proj(n) for n in range(n_ret)]
    cos = cos_ref[...]
    sin = sin_ref[...]
    for hd in range(RET_HEADS):
        y = _retention_head(hd, rq, rk, rv, rg, cos, sin, dm_ref, qd_ref, kd_ref, bd_ref, state_ref)
        yret_ref[0, :, hd * RET_HEAD_DIM:(hd + 1) * RET_HEAD_DIM] = y.astype(yret_ref.dtype)
    for n in range(n_qkv):
        p = proj(n_ret + n)
        if n * PROJ_TN < D_SB:
            p = p * SB_Q_SCALE
        sb_ref[0, :, n * PROJ_TN:(n + 1) * PROJ_TN] = p.astype(BF16)
    sg_ref[0] = proj(n_ret + n_qkv)


def _inproj(x, g, scale, shift, w_in_bf16, tables):
    b, s, d = x.shape
    d_in = w_in_bf16.shape[1]
    cos2, sin2, dmask, q_dec, k_dec, blk = tables
    once = lambda a: pl.BlockSpec(a.shape, lambda bi, i: (0,) * a.ndim, pipeline_mode=pl.Buffered(1))
    return pl.pallas_call(
        _inproj_kernel,
        grid=(b, s // PROJ_TM),
        in_specs=[
            pl.BlockSpec((1, PROJ_TM, d), lambda bi, i: (bi, i, 0)),
            pl.BlockSpec((1, d), lambda bi, i: (0, 0)),
            pl.BlockSpec((1, 1, d), lambda bi, i: (bi, 0, 0)),
            pl.BlockSpec((1, 1, d), lambda bi, i: (bi, 0, 0)),
            once(w_in_bf16),
            pl.BlockSpec((PROJ_TM, LANES), lambda bi, i: (i, 0)),
            pl.BlockSpec((PROJ_TM, LANES), lambda bi, i: (i, 0)),
            once(dmask), once(q_dec), once(k_dec), once(blk),
        ],
        out_specs=[
            pl.BlockSpec((1, PROJ_TM, D_RET), lambda bi, i: (bi, i, 0)),
            pl.BlockSpec((1, PROJ_TM, 3 * D_SB), lambda bi, i: (bi, i, 0)),
            pl.BlockSpec((1, PROJ_TM, D_SB), lambda bi, i: (bi, i, 0)),
        ],
        out_shape=[
            jax.ShapeDtypeStruct((b, s, D_RET), BF16),
            jax.ShapeDtypeStruct((b, s, 3 * D_SB), BF16),
            jax.ShapeDtypeStruct((b, s, D_SB), F32),
        ],
        scratch_shapes=[pltpu.VMEM((RET_HEADS, RET_HEAD_DIM, RET_HEAD_DIM), F32)],
        compiler_params=pltpu.CompilerParams(vmem_limit_bytes=56 * MIB),
        name="in_proj_retention",
    )(x, g.reshape(1, d), scale.reshape(b, 1, d), shift.reshape(b, 1, d), w_in_bf16,
      cos2, sin2, dmask, q_dec, k_dec, blk)


def _softplus2(u):
    ln = jnp.log(1.0 + jnp.exp2(jnp.minimum(u, SB_EXP2_CLAMP)))
    return jnp.maximum(u, ln * LOG2E_HI + ln * LOG2E_LO)


def _sb_scores(qm_ref, k_ref, u_ref, sp_ref, sub, slot, key0, diagonal, bias=None):
    for pair in range(SB_PAIRS):
        kb = k_ref[0, pl.ds(key0, SB_TK), pair * LANES:(pair + 1) * LANES]
        qm = qm_ref[sub, pair].reshape(2 * SB_TK, LANES)
        u = lax.dot_general(qm, kb, (((1,), (1,)), ((), ())), preferred_element_type=F32)
        if bias is not None:
            u = u + bias
        sp = _softplus2(u.astype(BF16))
        if diagonal:
            t_loc = lax.broadcasted_iota(jnp.int32, (SB_TK, SB_TK), 0)
            s_loc = lax.broadcasted_iota(jnp.int32, (SB_TK, SB_TK), 1)
            m = s_loc < t_loc
            mask = jnp.concatenate([m, m], axis=0)
            sp = jnp.where(mask, sp, jnp.zeros_like(sp))
            u = jnp.where(mask, u, SB_MASKED)
        u_ref[slot, pair] = u.reshape(2, SB_TK, SB_TK)
        sp_ref[slot, pair] = sp.reshape(2, SB_TK, SB_TK)


def _sb_values(v_ref, tri_ref, u_ref, sp_ref, acc_ref, ncarry_ref, sub, slot, key0):
    for pair in range(SB_PAIRS):
        vb = v_ref[0, pl.ds(key0, SB_TK), pair * LANES:(pair + 1) * LANES]
        sp = sp_ref[slot, pair].reshape(2 * SB_TK, SB_TK)
        r = jnp.dot(sp, tri_ref[...], preferred_element_type=F32)
        u = u_ref[slot, pair].reshape(2 * SB_TK, SB_TK)
        ncarry = ncarry_ref[sub, pair].reshape(2 * SB_TK, LANES)
        a = jnp.exp2(u - r + jnp.concatenate([ncarry] * (SB_TK // LANES), axis=1))
        pv = jnp.dot(a.astype(BF16), vb, preferred_element_type=F32)
        acc_ref[sub, pair] += pv.reshape(2, SB_TK, LANES)
        ncarry_ref[sub, pair] = (ncarry - r[:, :1]).reshape(2, SB_TK, LANES)


def _sb_kernel(q_ref, k_ref, v_ref, g_ref, tri_ref, o_ref,
               qm_ref, acc_ref, ncarry_ref, u_ref, sp_ref):
    step = pl.program_id(1)
    lane = lax.broadcasted_iota(jnp.int32, (SB_TK, LANES), 1)
    first = lane < SB_HEAD_DIM
    for sub in range(SB_SUB):
        for pair in range(SB_PAIRS):
            q = q_ref[0, sub * SB_TK:(sub + 1) * SB_TK, pair * LANES:(pair + 1) * LANES]
            zero = jnp.zeros_like(q)
            qm_ref[sub, pair, 0] = jnp.where(first, q, zero)
            qm_ref[sub, pair, 1] = jnp.where(first, zero, q)
    acc_ref[...] = jnp.zeros_like(acc_ref)
    ncarry_ref[...] = jnp.zeros_like(ncarry_ref)

    scores = functools.partial(_sb_scores, qm_ref, k_ref, u_ref, sp_ref)
    values = functools.partial(_sb_values, v_ref, tri_ref, u_ref, sp_ref, acc_ref, ncarry_ref)
    key_of = lambda blk: pl.multiple_of(blk * SB_TK, SB_TK)

    own = lambda sub: step * SB_SUB + sub
    no_prev = jnp.where(step == 0, SB_MASKED, 0.0)
    sequence = []
    for sub in range(SB_SUB):
        sequence.append((sub, own(sub), True, None))
        prev = jnp.maximum(own(sub) - 1, 0) if sub == 0 else own(sub) - 1
        sequence.append((sub, prev, False, no_prev if sub == 0 else None))
    for n, (sub, blk, diagonal, bias) in enumerate(sequence):
        scores(sub, n % 2, key_of(blk), diagonal, bias)
        if n > 0:
            p_sub, p_blk, _, _ = sequence[n - 1]
            values(p_sub, (n - 1) % 2, key_of(p_blk))
    p_sub, p_blk, _, _ = sequence[-1]
    values(p_sub, (len(sequence) - 1) % 2, key_of(p_blk))

    live = lambda sub: (jnp.max(ncarry_ref[sub]) > -SB_DEAD).astype(jnp.int32)
    live_now = [live(sub) for sub in range(SB_SUB)]
    for sub in range(SB_SUB):
        def body(state):
            blk, _ = state
            scores(sub, 0, key_of(blk), False)
            values(sub, 0, key_of(blk))
            return blk - 1, live(sub)

        lax.while_loop(lambda state: jnp.logical_and(state[0] >= 0, state[1] > 0),
                       body, (own(sub) - 2, live_now[sub]))

    for sub in range(SB_SUB):
        rows = slice(sub * SB_TK, (sub + 1) * SB_TK)
        for pair in range(SB_PAIRS):
            lanes = slice(pair * LANES, (pair + 1) * LANES)
            y = jnp.where(first, acc_ref[sub, pair, 0], acc_ref[sub, pair, 1])
            o_ref[0, rows, lanes] = (y * _silu(g_ref[0, rows, lanes])).astype(o_ref.dtype)


def _stick_breaking(sb, sg, tri):
    b, s, _ = sb.shape
    resident = lambda col: pl.BlockSpec((1, s, D_SB), lambda bi, i: (bi, 0, col))
    return pl.pallas_call(
        _sb_kernel,
        grid=(b, s // SB_TQ),
        in_specs=[
            pl.BlockSpec((1, SB_TQ, D_SB), lambda bi, i: (bi, i, 0)),
            resident(1),
            resident(2),
            pl.BlockSpec((1, SB_TQ, D_SB), lambda bi, i: (bi, i, 0)),
            pl.BlockSpec(tri.shape, lambda bi, i: (0, 0)),
        ],
        out_specs=pl.BlockSpec((1, SB_TQ, D_SB), lambda bi, i: (bi, i, 0)),
        out_shape=jax.ShapeDtypeStruct((b, s, D_SB), BF16),
        scratch_shapes=[
            pltpu.VMEM((SB_SUB, SB_PAIRS, 2, SB_TK, LANES), BF16),
            pltpu.VMEM((SB_SUB, SB_PAIRS, 2, SB_TK, LANES), F32),
            pltpu.VMEM((SB_SUB, SB_PAIRS, 2, SB_TK, LANES), F32),
            pltpu.VMEM((2, SB_PAIRS, 2, SB_TK, SB_TK), F32),
            pltpu.VMEM((2, SB_PAIRS, 2, SB_TK, SB_TK), BF16),
        ],
        compiler_params=pltpu.CompilerParams(vmem_limit_bytes=56 * MIB),
        name="stick_breaking",
    )(sb, sb, sb, sg, tri)


def _outproj_kernel(yr_ref, ys_ref, w_ref, x_ref, gate_ref, fg_ref, o_ref, *, final):
    y = jnp.dot(yr_ref[0], w_ref[:D_RET, :], preferred_element_type=F32)
    y = y + jnp.dot(ys_ref[0], w_ref[D_RET:, :], preferred_element_type=F32)
    xn = x_ref[0] + gate_ref[0] * y
    if final:
        inv = lax.rsqrt(jnp.mean(xn * xn, axis=-1, keepdims=True) + EPS)
        xn = xn * inv * fg_ref[...]
    o_ref[0] = xn.astype(o_ref.dtype)


def _outproj(y_ret, y_sb, w_out_bf16, x, gate, final_g, final):
    b, s, d = x.shape
    return pl.pallas_call(
        functools.partial(_outproj_kernel, final=final),
        grid=(b, s // OUT_TM),
        in_specs=[
            pl.BlockSpec((1, OUT_TM, D_RET), lambda bi, i: (bi, i, 0)),
            pl.BlockSpec((1, OUT_TM, D_SB), lambda bi, i: (bi, i, 0)),
            pl.BlockSpec(w_out_bf16.shape, lambda bi, i: (0, 0)),
            pl.BlockSpec((1, OUT_TM, d), lambda bi, i: (bi, i, 0)),
            pl.BlockSpec((1, 1, d), lambda bi, i: (bi, 0, 0)),
            pl.BlockSpec((1, d), lambda bi, i: (0, 0)),
        ],
        out_specs=pl.BlockSpec((1, OUT_TM, d), lambda bi, i: (bi, i, 0)),
        out_shape=jax.ShapeDtypeStruct((b, s, d), x.dtype),
        compiler_params=pltpu.CompilerParams(vmem_limit_bytes=40 * MIB),
        name="out_proj_final" if final else "out_proj",
    )(y_ret, y_sb, w_out_bf16, x, gate.reshape(b, 1, d), final_g.reshape(1, d))


def kernel(x, c, norm_g, w_ada, b_ada, w_in, w_out, final_g):
    b, s, d = x.shape
    assert d == D_MODEL and s % max(SB_TQ, RET_L, PROJ_TM, OUT_TM) == 0
    c_pad = jnp.zeros((ADA_ROWS, d), F32).at[:b].set(c.astype(F32))
    mod = _ada(c_pad, w_ada, b_ada)[:, :b]
    tables = _retention_tables(s)
    tri = jnp.asarray(np.tril(np.ones((SB_TK, SB_TK))), dtype=BF16)
    w_in_b = w_in.astype(BF16)
    w_out_b = w_out.astype(BF16)

    h = x.astype(F32)
    for layer in range(DEPTH):
        shift, scale, gate = jnp.split(mod[layer], 3, axis=-1)
        y_ret, sb, sg = _inproj(h, norm_g[layer], scale, shift, w_in_b[layer], tables)
        y_sb = _stick_breaking(sb, sg, tri)
        h = _outproj(y_ret, y_sb, w_out_b[layer], h, gate, final_g, layer == DEPTH - 1)
    return h.astype(x.dtype)
```

```python
import functools

import jax
import jax.numpy as jnp
import numpy as np
from jax import lax
from jax.experimental import pallas as pl
from jax.experimental.pallas import tpu as pltpu

D_MODEL = 1024
DEPTH = 2
CHUNK = 64
D_RET = 512
D_SB = 512
RET_HEADS = 4
RET_HEAD_DIM = 128
SB_HEADS = 8
SB_HEAD_DIM = 64
ROPE_BASE = 10000.0
EPS = 1e-6
LOG2E = 1.4426950408889634
LOG2E_HI = 1.4453125
LOG2E_LO = LOG2E - LOG2E_HI

LANES = 128
SB_PAIRS = SB_HEADS * SB_HEAD_DIM // LANES
MIB = 1024 * 1024

ADA_ROWS = 8
ADA_TN = 768
PROJ_TN = 512
RET_L = 256
PROJ_TM = 2 * RET_L
SB_TK = 256
SB_SUB = 2
SB_TQ = SB_SUB * SB_TK
OUT_TM = 1024

SB_EXP2_CLAMP = 126.0
SB_DEAD = 151.0
SB_MASKED = -1e30
SB_Q_SCALE = LOG2E * SB_HEAD_DIM ** -0.5

F32 = jnp.float32
BF16 = jnp.bfloat16


def _silu(x):
    return x * (1.0 / (1.0 + jnp.exp(-x)))


def _ada_kernel(c_ref, w_ref, b_ref, o_ref):
    c_act = _silu(c_ref[...])
    o_ref[0] = jnp.dot(c_act, w_ref[0], preferred_element_type=F32,
                       precision=lax.Precision.HIGHEST) + b_ref[0]


def _ada(c_pad, w_ada, b_ada):
    n = 3 * D_MODEL
    return pl.pallas_call(
        _ada_kernel,
        grid=(DEPTH, n // ADA_TN),
        in_specs=[
            pl.BlockSpec((ADA_ROWS, D_MODEL), lambda l, j: (0, 0)),
            pl.BlockSpec((1, D_MODEL, ADA_TN), lambda l, j: (l, 0, j)),
            pl.BlockSpec((1, 1, ADA_TN), lambda l, j: (l, 0, j)),
        ],
        out_specs=pl.BlockSpec((1, ADA_ROWS, ADA_TN), lambda l, j: (l, 0, j)),
        out_shape=jax.ShapeDtypeStruct((DEPTH, ADA_ROWS, n), F32),
        name="ada_mod",
    )(c_pad, w_ada, b_ada.reshape(DEPTH, 1, n))


def _retention_tables(seq):
    half = RET_HEAD_DIM // 2
    pos = np.arange(seq, dtype=np.float64)
    inv = ROPE_BASE ** (-np.arange(half, dtype=np.float64) / half)
    ang = pos[:, None] * inv[None, :]
    cos, sin = np.cos(ang), np.sin(ang)
    cos2 = np.concatenate([cos, cos], axis=-1)
    sin2 = np.concatenate([-sin, sin], axis=-1)

    log_gamma = np.log1p(-(2.0 ** (-5.0 - np.arange(RET_HEADS, dtype=np.float64))))
    idx = np.arange(RET_L, dtype=np.float64)
    diff = idx[:, None] - idx[None, :]
    same = (np.arange(RET_L)[:, None] // CHUNK) == (np.arange(RET_L)[None, :] // CHUNK)
    expo = np.where(same, np.abs(diff), diff)
    dmask = np.where(same | (diff > 0), np.exp(expo[None] * log_gamma[:, None, None]), 0.0)
    q_dec = np.exp((idx + 1.0)[None, :] * log_gamma[:, None])
    k_dec = np.exp((RET_L - 1.0 - idx)[None, :] * log_gamma[:, None])
    blk_dec = np.exp(RET_L * log_gamma)
    rep = lambda t: np.broadcast_to(t[:, :, None], (RET_HEADS, RET_L, LANES))
    blk = np.broadcast_to(blk_dec[:, None, None], (RET_HEADS, 1, LANES))
    return tuple(jnp.asarray(t, dtype=F32) for t in (cos2, sin2, dmask, rep(q_dec), rep(k_dec), blk))


def _retention_head(h, rq, rk, rv, rg, cos, sin, dm_ref, qd_ref, kd_ref, bd_ref, state_ref):
    half = RET_HEAD_DIM // 2
    cols = slice(h * RET_HEAD_DIM, (h + 1) * RET_HEAD_DIM)
    q = rq[:, cols]
    k = rk[:, cols]
    q = q * cos + pltpu.roll(q, half, axis=1) * sin
    k = (k * cos + pltpu.roll(k, half, axis=1) * sin) * (RET_HEAD_DIM ** -0.5)
    vb = rv[:, cols].astype(BF16)
    qb = q.astype(BF16)
    scores = lax.dot_general(qb, k.astype(BF16), (((1,), (1,)), ((), ())),
                             preferred_element_type=F32) * dm_ref[h]
    o = jnp.dot(scores.astype(BF16), vb, preferred_element_type=F32)
    state = state_ref[h]
    o = o + jnp.dot(qb, state.astype(BF16), preferred_element_type=F32) * qd_ref[h]
    kv = lax.dot_general((k * kd_ref[h]).astype(BF16), vb, (((0,), (0,)), ((), ())),
                         preferred_element_type=F32)
    state_ref[h] = state * bd_ref[h] + kv
    mu = jnp.mean(o, axis=-1, keepdims=True)
    oc = o - mu
    var = jnp.mean(oc * oc, axis=-1, keepdims=True)
    return oc * lax.rsqrt(var + EPS) * _silu(rg[:, cols])


def _inproj_kernel(x_ref, g_ref, scale_ref, shift_ref, w_ref, cos_ref, sin_ref, dm_ref, qd_ref,
                   kd_ref, bd_ref, yret_ref, sb_ref, sg_ref, state_ref):
    @pl.when(pl.program_id(1) == 0)
    def _():
        state_ref[...] = jnp.zeros_like(state_ref)

    x = x_ref[0]
    inv = lax.rsqrt(jnp.mean(x * x, axis=-1, keepdims=True) + EPS)
    h = (x * inv * g_ref[...]) * (1.0 + scale_ref[0]) + shift_ref[0]
    hb = h.astype(BF16)
    proj = lambda n: jnp.dot(hb, w_ref[:, n * PROJ_TN:(n + 1) * PROJ_TN], preferred_element_type=F32)
    n_ret = 4 * D_RET // PROJ_TN
    n_qkv = 3 * D_SB // PROJ_TN
    ret = [proj(n) for n in range(n_ret)]
    for blk in range(PROJ_TM // RET_L):
        rows = slice(blk * RET_L, (blk + 1) * RET_L)
        rq, rk, rv, rg = [r[rows] for r in ret]
        cos = cos_ref[rows, :]
        sin = sin_ref[rows, :]
        for hd in range(RET_HEADS):
            y = _retention_head(hd, rq, rk, rv, rg, cos, sin, dm_ref, qd_ref, kd_ref, bd_ref,
                                state_ref)
            yret_ref[0, rows, hd * RET_HEAD_DIM:(hd + 1) * RET_HEAD_DIM] = y.astype(yret_ref.dtype)
    for n in range(n_qkv):
        p = proj(n_ret + n)
        if n * PROJ_TN < D_SB:
            p = p * SB_Q_SCALE
        sb_ref[0, :, n * PROJ_TN:(n + 1) * PROJ_TN] = p.astype(BF16)
    sg_ref[0] = proj(n_ret + n_qkv)


def _inproj(x, g, scale, shift, w_in_bf16, tables):
    b, s, d = x.shape
    d_in = w_in_bf16.shape[1]
    cos2, sin2, dmask, q_dec, k_dec, blk = tables
    once = lambda a: pl.BlockSpec(a.shape, lambda bi, i: (0,) * a.ndim, pipeline_mode=pl.Buffered(1))
    return pl.pallas_call(
        _inproj_kernel,
        grid=(b, s // PROJ_TM),
        in_specs=[
            pl.BlockSpec((1, PROJ_TM, d), lambda bi, i: (bi, i, 0)),
            pl.BlockSpec((1, d), lambda bi, i: (0, 0)),
            pl.BlockSpec((1, 1, d), lambda bi, i: (bi, 0, 0)),
            pl.BlockSpec((1, 1, d), lambda bi, i: (bi, 0, 0)),
            once(w_in_bf16),
            pl.BlockSpec((PROJ_TM, LANES), lambda bi, i: (i, 0)),
            pl.BlockSpec((PROJ_TM, LANES), lambda bi, i: (i, 0)),
            once(dmask), once(q_dec), once(k_dec), once(blk),
        ],
        out_specs=[
            pl.BlockSpec((1, PROJ_TM, D_RET), lambda bi, i: (bi, i, 0)),
            pl.BlockSpec((1, PROJ_TM, 3 * D_SB), lambda bi, i: (bi, i, 0)),
            pl.BlockSpec((1, PROJ_TM, D_SB), lambda bi, i: (bi, i, 0)),
        ],
        out_shape=[
            jax.ShapeDtypeStruct((b, s, D_RET), BF16),
            jax.ShapeDtypeStruct((b, s, 3 * D_SB), BF16),
            jax.ShapeDtypeStruct((b, s, D_SB), F32),
        ],
        scratch_shapes=[pltpu.VMEM((RET_HEADS, RET_HEAD_DIM, RET_HEAD_DIM), F32)],
        compiler_params=pltpu.CompilerParams(vmem_limit_bytes=56 * MIB),
        name="in_proj_retention",
    )(x, g.reshape(1, d), scale.reshape(b, 1, d), shift.reshape(b, 1, d), w_in_bf16,
      cos2, sin2, dmask, q_dec, k_dec, blk)


def _softplus2(u):
    ln = jnp.log(1.0 + jnp.exp2(jnp.minimum(u, SB_EXP2_CLAMP)))
    return jnp.maximum(u, ln * LOG2E_HI + ln * LOG2E_LO)


def _sb_scores(qm_ref, k_ref, u_ref, sp_ref, sub, slot, key0, diagonal, bias=None):
    for pair in range(SB_PAIRS):
        kb = k_ref[0, pl.ds(key0, SB_TK), pair * LANES:(pair + 1) * LANES]
        qm = qm_ref[sub, pair].reshape(2 * SB_TK, LANES)
        u = lax.dot_general(qm, kb, (((1,), (1,)), ((), ())), preferred_element_type=F32)
        if bias is not None:
            u = u + bias
        sp = _softplus2(u.astype(BF16))
        if diagonal:
            t_loc = lax.broadcasted_iota(jnp.int32, (SB_TK, SB_TK), 0)
            s_loc = lax.broadcasted_iota(jnp.int32, (SB_TK, SB_TK), 1)
            m = s_loc < t_loc
            mask = jnp.concatenate([m, m], axis=0)
            sp = jnp.where(mask, sp, jnp.zeros_like(sp))
            u = jnp.where(mask, u, SB_MASKED)
        u_ref[slot, pair] = u.reshape(2, SB_TK, SB_TK)
        sp_ref[slot, pair] = sp.reshape(2, SB_TK, SB_TK)


def _sb_values(v_ref, tri_ref, u_ref, sp_ref, acc_ref, ncarry_ref, sub, slot, key0):
    for pair in range(SB_PAIRS):
        vb = v_ref[0, pl.ds(key0, SB_TK), pair * LANES:(pair + 1) * LANES]
        sp = sp_ref[slot, pair].reshape(2 * SB_TK, SB_TK)
        r = jnp.dot(sp, tri_ref[...], preferred_element_type=F32)
        u = u_ref[slot, pair].reshape(2 * SB_TK, SB_TK)
        ncarry = ncarry_ref[sub, pair].reshape(2 * SB_TK, LANES)
        a = jnp.exp2(u - r + jnp.concatenate([ncarry] * (SB_TK // LANES), axis=1))
        pv = jnp.dot(a.astype(BF16), vb, preferred_element_type=F32)
        acc_ref[sub, pair] += pv.reshape(2, SB_TK, LANES)
        ncarry_ref[sub, pair] = (ncarry - r[:, :1]).reshape(2, SB_TK, LANES)


def _sb_kernel(q_ref, k_ref, v_ref, g_ref, tri_ref, o_ref,
               qm_ref, acc_ref, ncarry_ref, u_ref, sp_ref):
    step = pl.program_id(1)
    lane = lax.broadcasted_iota(jnp.int32, (SB_TK, LANES), 1)
    first = lane < SB_HEAD_DIM
    for sub in range(SB_SUB):
        for pair in range(SB_PAIRS):
            q = q_ref[0, sub * SB_TK:(sub + 1) * SB_TK, pair * LANES:(pair + 1) * LANES]
            zero = jnp.zeros_like(q)
            qm_ref[sub, pair, 0] = jnp.where(first, q, zero)
            qm_ref[sub, pair, 1] = jnp.where(first, zero, q)
    acc_ref[...] = jnp.zeros_like(acc_ref)
    ncarry_ref[...] = jnp.zeros_like(ncarry_ref)

    scores = functools.partial(_sb_scores, qm_ref, k_ref, u_ref, sp_ref)
    values = functools.partial(_sb_values, v_ref, tri_ref, u_ref, sp_ref, acc_ref, ncarry_ref)
    key_of = lambda blk: pl.multiple_of(blk * SB_TK, SB_TK)

    own = lambda sub: step * SB_SUB + sub
    no_prev = jnp.where(step == 0, SB_MASKED, 0.0)
    sequence = []
    for sub in range(SB_SUB):
        sequence.append((sub, own(sub), True, None))
        prev = jnp.maximum(own(sub) - 1, 0) if sub == 0 else own(sub) - 1
        sequence.append((sub, prev, False, no_prev if sub == 0 else None))
    for n, (sub, blk, diagonal, bias) in enumerate(sequence):
        scores(sub, n % 2, key_of(blk), diagonal, bias)
        if n > 0:
            p_sub, p_blk, _, _ = sequence[n - 1]
            values(p_sub, (n - 1) % 2, key_of(p_blk))
    p_sub, p_blk, _, _ = sequence[-1]
    values(p_sub, (len(sequence) - 1) % 2, key_of(p_blk))

    live = lambda sub: (jnp.max(ncarry_ref[sub]) > -SB_DEAD).astype(jnp.int32)
    live_now = [live(sub) for sub in range(SB_SUB)]
    for sub in range(SB_SUB):
        def body(state):
            blk, _ = state
            scores(sub, 0, key_of(blk), False)
            values(sub, 0, key_of(blk))
            return blk - 1, live(sub)

        lax.while_loop(lambda state: jnp.logical_and(state[0] >= 0, state[1] > 0),
                       body, (own(sub) - 2, live_now[sub]))

    for sub in range(SB_SUB):
        rows = slice(sub * SB_TK, (sub + 1) * SB_TK)
        for pair in range(SB_PAIRS):
            lanes = slice(pair * LANES, (pair + 1) * LANES)
            y = jnp.where(first, acc_ref[sub, pair, 0], acc_ref[sub, pair, 1])
            o_ref[0, rows, lanes] = (y * _silu(g_ref[0, rows, lanes])).astype(o_ref.dtype)


def _stick_breaking(sb, sg, tri):
    b, s, _ = sb.shape
    resident = lambda col: pl.BlockSpec((1, s, D_SB), lambda bi, i: (bi, 0, col))
    return pl.pallas_call(
        _sb_kernel,
        grid=(b, s // SB_TQ),
        in_specs=[
            pl.BlockSpec((1, SB_TQ, D_SB), lambda bi, i: (bi, i, 0)),
            resident(1),
            resident(2),
            pl.BlockSpec((1, SB_TQ, D_SB), lambda bi, i: (bi, i, 0)),
            pl.BlockSpec(tri.shape, lambda bi, i: (0, 0)),
        ],
        out_specs=pl.BlockSpec((1, SB_TQ, D_SB), lambda bi, i: (bi, i, 0)),
        out_shape=jax.ShapeDtypeStruct((b, s, D_SB), BF16),
        scratch_shapes=[
            pltpu.VMEM((SB_SUB, SB_PAIRS, 2, SB_TK, LANES), BF16),
            pltpu.VMEM((SB_SUB, SB_PAIRS, 2, SB_TK, LANES), F32),
            pltpu.VMEM((SB_SUB, SB_PAIRS, 2, SB_TK, LANES), F32),
            pltpu.VMEM((2, SB_PAIRS, 2, SB_TK, SB_TK), F32),
            pltpu.VMEM((2, SB_PAIRS, 2, SB_TK, SB_TK), BF16),
        ],
        compiler_params=pltpu.CompilerParams(vmem_limit_bytes=56 * MIB),
        name="stick_breaking",
    )(sb, sb, sb, sg, tri)


def _outproj_kernel(yr_ref, ys_ref, w_ref, x_ref, gate_ref, fg_ref, o_ref, *, final):
    y = jnp.dot(yr_ref[0], w_ref[:D_RET, :], preferred_element_type=F32)
    y = y + jnp.dot(ys_ref[0], w_ref[D_RET:, :], preferred_element_type=F32)
    xn = x_ref[0] + gate_ref[0] * y
    if final:
        inv = lax.rsqrt(jnp.mean(xn * xn, axis=-1, keepdims=True) + EPS)
        xn = xn * inv * fg_ref[...]
    o_ref[0] = xn.astype(o_ref.dtype)


def _outproj(y_ret, y_sb, w_out_bf16, x, gate, final_g, final):
    b, s, d = x.shape
    return pl.pallas_call(
        functools.partial(_outproj_kernel, final=final),
        grid=(b, s // OUT_TM),
        in_specs=[
            pl.BlockSpec((1, OUT_TM, D_RET), lambda bi, i: (bi, i, 0)),
            pl.BlockSpec((1, OUT_TM, D_SB), lambda bi, i: (bi, i, 0)),
            pl.BlockSpec(w_out_bf16.shape, lambda bi, i: (0, 0)),
            pl.BlockSpec((1, OUT_TM, d), lambda bi, i: (bi, i, 0)),
            pl.BlockSpec((1, 1, d), lambda bi, i: (bi, 0, 0)),
            pl.BlockSpec((1, d), lambda bi, i: (0, 0)),
        ],
        out_specs=pl.BlockSpec((1, OUT_TM, d), lambda bi, i: (bi, i, 0)),
        out_shape=jax.ShapeDtypeStruct((b, s, d), x.dtype),
        compiler_params=pltpu.CompilerParams(vmem_limit_bytes=40 * MIB),
        name="out_proj_final" if final else "out_proj",
    )(y_ret, y_sb, w_out_bf16, x, gate.reshape(b, 1, d), final_g.reshape(1, d))


def kernel(x, c, norm_g, w_ada, b_ada, w_in, w_out, final_g):
    b, s, d = x.shape
    assert d == D_MODEL and s % max(SB_TQ, RET_L, PROJ_TM, OUT_TM) == 0
    c_pad = jnp.zeros((ADA_ROWS, d), F32).at[:b].set(c.astype(F32))
    mod = _ada(c_pad, w_ada, b_ada)[:, :b]
    tables = _retention_tables(s)
    tri = jnp.asarray(np.tril(np.ones((SB_TK, SB_TK))), dtype=BF16)
    w_in_b = w_in.astype(BF16)
    w_out_b = w_out.astype(BF16)

    h = x.astype(F32)
    for layer in range(DEPTH):
        shift, scale, gate = jnp.split(mod[layer], 3, axis=-1)
        y_ret, sb, sg = _inproj(h, norm_g[layer], scale, shift, w_in_b[layer], tables)
        y_sb = _stick_breaking(sb, sg, tri)
        h = _outproj(y_ret, y_sb, w_out_b[layer], h, gate, final_g, layer == DEPTH - 1)
    return h.astype(x.dtype)
```

```python
import functools

import jax
import jax.numpy as jnp
import numpy as np
from jax import lax
from jax.experimental import pallas as pl
from jax.experimental.pallas import tpu as pltpu

D_MODEL = 1024
DEPTH = 2
CHUNK = 64
D_RET = 512
D_SB = 512
RET_HEADS = 4
RET_HEAD_DIM = 128
SB_HEADS = 8
SB_HEAD_DIM = 64
ROPE_BASE = 10000.0
EPS = 1e-6
LOG2E = 1.4426950408889634
LOG2E_HI = 1.4453125
LOG2E_LO = LOG2E - LOG2E_HI

LANES = 128
SB_PAIRS = SB_HEADS * SB_HEAD_DIM // LANES
MIB = 1024 * 1024

ADA_ROWS = 8
ADA_TN = 768
PROJ_TN = 512
RET_L = 256
PROJ_TM = 2 * RET_L
SB_TK = 256
SB_SUB = 2
SB_TQ = SB_SUB * SB_TK

SB_EXP2_CLAMP = 126.0
SB_DEAD = 151.0
SB_MASKED = -1e30
SB_Q_SCALE = LOG2E * SB_HEAD_DIM ** -0.5

F32 = jnp.float32
BF16 = jnp.bfloat16


def _silu(x):
    return x * (1.0 / (1.0 + jnp.exp(-x)))


def _ada_kernel(c_ref, w_ref, b_ref, o_ref):
    c_act = _silu(c_ref[...])
    o_ref[0] = jnp.dot(c_act, w_ref[0], preferred_element_type=F32,
                       precision=lax.Precision.HIGHEST) + b_ref[0]


def _ada(c_pad, w_ada, b_ada):
    n = 3 * D_MODEL
    return pl.pallas_call(
        _ada_kernel,
        grid=(DEPTH, n // ADA_TN),
        in_specs=[
            pl.BlockSpec((ADA_ROWS, D_MODEL), lambda l, j: (0, 0)),
            pl.BlockSpec((1, D_MODEL, ADA_TN), lambda l, j: (l, 0, j)),
            pl.BlockSpec((1, 1, ADA_TN), lambda l, j: (l, 0, j)),
        ],
        out_specs=pl.BlockSpec((1, ADA_ROWS, ADA_TN), lambda l, j: (l, 0, j)),
        out_shape=jax.ShapeDtypeStruct((DEPTH, ADA_ROWS, n), F32),
        name="ada_mod",
    )(c_pad, w_ada, b_ada.reshape(DEPTH, 1, n))


def _retention_tables(seq):
    half = RET_HEAD_DIM // 2
    pos = np.arange(seq, dtype=np.float64)
    inv = ROPE_BASE ** (-np.arange(half, dtype=np.float64) / half)
    ang = pos[:, None] * inv[None, :]
    cos, sin = np.cos(ang), np.sin(ang)
    cos2 = np.concatenate([cos, cos], axis=-1)
    sin2 = np.concatenate([-sin, sin], axis=-1)

    log_gamma = np.log1p(-(2.0 ** (-5.0 - np.arange(RET_HEADS, dtype=np.float64))))
    idx = np.arange(RET_L, dtype=np.float64)
    diff = idx[:, None] - idx[None, :]
    same = (np.arange(RET_L)[:, None] // CHUNK) == (np.arange(RET_L)[None, :] // CHUNK)
    expo = np.where(same, np.abs(diff), diff)
    dmask = np.where(same | (diff > 0), np.exp(expo[None] * log_gamma[:, None, None]), 0.0)
    q_dec = np.exp((idx + 1.0)[None, :] * log_gamma[:, None])
    k_dec = np.exp((RET_L - 1.0 - idx)[None, :] * log_gamma[:, None])
    blk_dec = np.exp(RET_L * log_gamma)
    rep = lambda t: np.broadcast_to(t[:, :, None], (RET_HEADS, RET_L, LANES))
    blk = np.broadcast_to(blk_dec[:, None, None], (RET_HEADS, 1, LANES))
    return tuple(jnp.asarray(t, dtype=F32) for t in (cos2, sin2, dmask, rep(q_dec), rep(k_dec), blk))


def _retention_head(h, rq, rk, rv, rg, cos, sin, dm_ref, qd_ref, kd_ref, bd_ref, state_ref):
    half = RET_HEAD_DIM // 2
    cols = slice(h * RET_HEAD_DIM, (h + 1) * RET_HEAD_DIM)
    q = rq[:, cols]
    k = rk[:, cols]
    q = q * cos + pltpu.roll(q, half, axis=1) * sin
    k = (k * cos + pltpu.roll(k, half, axis=1) * sin) * (RET_HEAD_DIM ** -0.5)
    vb = rv[:, cols].astype(BF16)
    qb = q.astype(BF16)
    scores = lax.dot_general(qb, k.astype(BF16), (((1,), (1,)), ((), ())),
                             preferred_element_type=F32) * dm_ref[h]
    o = jnp.dot(scores.astype(BF16), vb, preferred_element_type=F32)
    state = state_ref[h]
    o = o + jnp.dot(qb, state.astype(BF16), preferred_element_type=F32) * qd_ref[h]
    kv = lax.dot_general((k * kd_ref[h]).astype(BF16), vb, (((0,), (0,)), ((), ())),
                         preferred_element_type=F32)
    state_ref[h] = state * bd_ref[h] + kv
    mu = jnp.mean(o, axis=-1, keepdims=True)
    oc = o - mu
    var = jnp.mean(oc * oc, axis=-1, keepdims=True)
    return oc * lax.rsqrt(var + EPS) * _silu(rg[:, cols])


def _inproj_kernel(x_ref, g_ref, scale_ref, shift_ref, w_ref, cos_ref, sin_ref, dm_ref, qd_ref,
                   kd_ref, bd_ref, yret_ref, sb_ref, sg_ref, state_ref):
    @pl.when(pl.program_id(1) == 0)
    def _():
        state_ref[...] = jnp.zeros_like(state_ref)

    x = x_ref[0]
    inv = lax.rsqrt(jnp.mean(x * x, axis=-1, keepdims=True) + EPS)
    h = (x * inv * g_ref[...]) * (1.0 + scale_ref[0]) + shift_ref[0]
    hb = h.astype(BF16)
    proj = lambda n: jnp.dot(hb, w_ref[:, n * PROJ_TN:(n + 1) * PROJ_TN], preferred_element_type=F32)
    n_ret = 4 * D_RET // PROJ_TN
    n_qkv = 3 * D_SB // PROJ_TN
    ret = [proj(n) for n in range(n_ret)]
    for blk in range(PROJ_TM // RET_L):
        rows = slice(blk * RET_L, (blk + 1) * RET_L)
        rq, rk, rv, rg = [r[rows] for r in ret]
        cos = cos_ref[rows, :]
        sin = sin_ref[rows, :]
        for hd in range(RET_HEADS):
            y = _retention_head(hd, rq, rk, rv, rg, cos, sin, dm_ref, qd_ref, kd_ref, bd_ref,
                                state_ref)
            yret_ref[0, rows, hd * RET_HEAD_DIM:(hd + 1) * RET_HEAD_DIM] = y.astype(yret_ref.dtype)
    for n in range(n_qkv):
        p = proj(n_ret + n)
        if n * PROJ_TN < D_SB:
            p = p * SB_Q_SCALE
        sb_ref[0, :, n * PROJ_TN:(n + 1) * PROJ_TN] = p.astype(BF16)
    sg_ref[0] = proj(n_ret + n_qkv)


def _inproj(x, g, scale, shift, w_in_bf16, tables):
    b, s, d = x.shape
    d_in = w_in_bf16.shape[1]
    cos2, sin2, dmask, q_dec, k_dec, blk = tables
    once = lambda a: pl.BlockSpec(a.shape, lambda bi, i: (0,) * a.ndim, pipeline_mode=pl.Buffered(1))
    return pl.pallas_call(
        _inproj_kernel,
        grid=(b, s // PROJ_TM),
        in_specs=[
            pl.BlockSpec((1, PROJ_TM, d), lambda bi, i: (bi, i, 0)),
            pl.BlockSpec((1, d), lambda bi, i: (0, 0)),
            pl.BlockSpec((1, 1, d), lambda bi, i: (bi, 0, 0)),
            pl.BlockSpec((1, 1, d), lambda bi, i: (bi, 0, 0)),
            once(w_in_bf16),
            pl.BlockSpec((PROJ_TM, LANES), lambda bi, i: (i, 0)),
            pl.BlockSpec((PROJ_TM, LANES), lambda bi, i: (i, 0)),
            once(dmask), once(q_dec), once(k_dec), once(blk),
        ],
        out_specs=[
            pl.BlockSpec((1, PROJ_TM, D_RET), lambda bi, i: (bi, i, 0)),
            pl.BlockSpec((1, PROJ_TM, 3 * D_SB), lambda bi, i: (bi, i, 0)),
            pl.BlockSpec((1, PROJ_TM, D_SB), lambda bi, i: (bi, i, 0)),
        ],
        out_shape=[
            jax.ShapeDtypeStruct((b, s, D_RET), BF16),
            jax.ShapeDtypeStruct((b, s, 3 * D_SB), BF16),
            jax.ShapeDtypeStruct((b, s, D_SB), F32),
        ],
        scratch_shapes=[pltpu.VMEM((RET_HEADS, RET_HEAD_DIM, RET_HEAD_DIM), F32)],
        compiler_params=pltpu.CompilerParams(vmem_limit_bytes=56 * MIB),
        name="in_proj_retention",
    )(x, g.reshape(1, d), scale.reshape(b, 1, d), shift.reshape(b, 1, d), w_in_bf16,
      cos2, sin2, dmask, q_dec, k_dec, blk)


def _softplus2(u):
    ln = jnp.log(1.0 + jnp.exp2(jnp.minimum(u, SB_EXP2_CLAMP)))
    return jnp.maximum(u, ln * LOG2E_HI + ln * LOG2E_LO)


def _sb_scores(qm_ref, k_ref, u_ref, sp_ref, sub, slot, key0, diagonal, bias=None):
    for pair in range(SB_PAIRS):
        kb = k_ref[0, pl.ds(key0, SB_TK), pair * LANES:(pair + 1) * LANES]
        qm = qm_ref[sub, pair].reshape(2 * SB_TK, LANES)
        u = lax.dot_general(qm, kb, (((1,), (1,)), ((), ())), preferred_element_type=F32)
        if bias is not None:
            u = u + bias
        sp = _softplus2(u.astype(BF16))
        if diagonal:
            t_loc = lax.broadcasted_iota(jnp.int32, (SB_TK, SB_TK), 0)
            s_loc = lax.broadcasted_iota(jnp.int32, (SB_TK, SB_TK), 1)
            m = s_loc < t_loc
            mask = jnp.concatenate([m, m], axis=0)
            sp = jnp.where(mask, sp, jnp.zeros_like(sp))
            u = jnp.where(mask, u, SB_MASKED)
        u_ref[slot, pair] = u.reshape(2, SB_TK, SB_TK)
        sp_ref[slot, pair] = sp.reshape(2, SB_TK, SB_TK)


def _sb_values(v_ref, tri_ref, u_ref, sp_ref, acc_ref, ncarry_ref, sub, slot, key0):
    for pair in range(SB_PAIRS):
        vb = v_ref[0, pl.ds(key0, SB_TK), pair * LANES:(pair + 1) * LANES]
        sp = sp_ref[slot, pair].reshape(2 * SB_TK, SB_TK)
        r = jnp.dot(sp, tri_ref[...], preferred_element_type=F32)
        u = u_ref[slot, pair].reshape(2 * SB_TK, SB_TK)
        ncarry = ncarry_ref[sub, pair].reshape(2 * SB_TK, LANES)
        a = jnp.exp2(u - r + jnp.concatenate([ncarry] * (SB_TK // LANES), axis=1))
        pv = jnp.dot(a.astype(BF16), vb, preferred_element_type=F32)
        acc_ref[sub, pair] += pv.reshape(2, SB_TK, LANES)
        ncarry_ref[sub, pair] = (ncarry - r[:, :1]).reshape(2, SB_TK, LANES)


def _sb_kernel(q_ref, k_ref, v_ref, g_ref, tri_ref, yr_ref, w_ref, x_ref, gate_ref, fg_ref, o_ref,
               qm_ref, acc_ref, ncarry_ref, u_ref, sp_ref, ys_ref, res_ref, *, final):
    step = pl.program_id(1)
    res_ref[...] = jnp.dot(yr_ref[0], w_ref[:D_RET, :], preferred_element_type=F32)
    lane = lax.broadcasted_iota(jnp.int32, (SB_TK, LANES), 1)
    first = lane < SB_HEAD_DIM
    for sub in range(SB_SUB):
        for pair in range(SB_PAIRS):
            q = q_ref[0, sub * SB_TK:(sub + 1) * SB_TK, pair * LANES:(pair + 1) * LANES]
            zero = jnp.zeros_like(q)
            qm_ref[sub, pair, 0] = jnp.where(first, q, zero)
            qm_ref[sub, pair, 1] = jnp.where(first, zero, q)
    acc_ref[...] = jnp.zeros_like(acc_ref)
    ncarry_ref[...] = jnp.zeros_like(ncarry_ref)

    scores = functools.partial(_sb_scores, qm_ref, k_ref, u_ref, sp_ref)
    values = functools.partial(_sb_values, v_ref, tri_ref, u_ref, sp_ref, acc_ref, ncarry_ref)
    key_of = lambda blk: pl.multiple_of(blk * SB_TK, SB_TK)

    own = lambda sub: step * SB_SUB + sub
    no_prev = jnp.where(step == 0, SB_MASKED, 0.0)
    sequence = []
    for sub in range(SB_SUB):
        sequence.append((sub, own(sub), True, None))
        prev = jnp.maximum(own(sub) - 1, 0) if sub == 0 else own(sub) - 1
        sequence.append((sub, prev, False, no_prev if sub == 0 else None))
    for n, (sub, blk, diagonal, bias) in enumerate(sequence):
        scores(sub, n % 2, key_of(blk), diagonal, bias)
        if n > 0:
            p_sub, p_blk, _, _ = sequence[n - 1]
            values(p_sub, (n - 1) % 2, key_of(p_blk))
    p_sub, p_blk, _, _ = sequence[-1]
    values(p_sub, (len(sequence) - 1) % 2, key_of(p_blk))

    live = lambda sub: (jnp.max(ncarry_ref[sub]) > -SB_DEAD).astype(jnp.int32)
    live_now = [live(sub) for sub in range(SB_SUB)]
    for sub in range(SB_SUB):
        def body(state):
            blk, _ = state
            scores(sub, 0, key_of(blk), False)
            values(sub, 0, key_of(blk))
            return blk - 1, live(sub)

        lax.while_loop(lambda state: jnp.logical_and(state[0] >= 0, state[1] > 0),
                       body, (own(sub) - 2, live_now[sub]))

    for sub in range(SB_SUB):
        rows = slice(sub * SB_TK, (sub + 1) * SB_TK)
        for pair in range(SB_PAIRS):
            lanes = slice(pair * LANES, (pair + 1) * LANES)
            y = jnp.where(first, acc_ref[sub, pair, 0], acc_ref[sub, pair, 1])
            ys_ref[rows, lanes] = (y * _silu(g_ref[0, rows, lanes])).astype(ys_ref.dtype)

    y = res_ref[...] + jnp.dot(ys_ref[...], w_ref[D_RET:, :], preferred_element_type=F32)
    xn = x_ref[0] + gate_ref[0] * y
    if final:
        inv = lax.rsqrt(jnp.mean(xn * xn, axis=-1, keepdims=True) + EPS)
        xn = xn * inv * fg_ref[...]
    o_ref[0] = xn.astype(o_ref.dtype)


def _stick_breaking_outproj(sb, sg, tri, y_ret, w_out_bf16, x, gate, final_g, final):
    b, s, d = x.shape
    rows = lambda width: pl.BlockSpec((1, SB_TQ, width), lambda bi, i: (bi, i, 0))
    once = lambda a: pl.BlockSpec(a.shape, lambda bi, i: (0,) * a.ndim, pipeline_mode=pl.Buffered(1))
    resident = lambda col: pl.BlockSpec((1, s, D_SB), lambda bi, i: (bi, 0, col),
                                        pipeline_mode=pl.Buffered(1))
    return pl.pallas_call(
        functools.partial(_sb_kernel, final=final),
        grid=(b, s // SB_TQ),
        in_specs=[
            rows(D_SB), resident(1), resident(2), rows(D_SB), once(tri),
            rows(D_RET), once(w_out_bf16), rows(d),
            pl.BlockSpec((1, 1, d), lambda bi, i: (bi, 0, 0)),
            pl.BlockSpec((1, d), lambda bi, i: (0, 0)),
        ],
        out_specs=rows(d),
        out_shape=jax.ShapeDtypeStruct((b, s, d), x.dtype),
        scratch_shapes=[
            pltpu.VMEM((SB_SUB, SB_PAIRS, 2, SB_TK, LANES), BF16),
            pltpu.VMEM((SB_SUB, SB_PAIRS, 2, SB_TK, LANES), F32),
            pltpu.VMEM((SB_SUB, SB_PAIRS, 2, SB_TK, LANES), F32),
            pltpu.VMEM((2, SB_PAIRS, 2, SB_TK, SB_TK), F32),
            pltpu.VMEM((2, SB_PAIRS, 2, SB_TK, SB_TK), BF16),
            pltpu.VMEM((SB_TQ, D_SB), BF16),
            pltpu.VMEM((SB_TQ, D_MODEL), F32),
        ],
        compiler_params=pltpu.CompilerParams(vmem_limit_bytes=56 * MIB),
        name="stick_breaking_out_proj_final" if final else "stick_breaking_out_proj",
    )(sb, sb, sb, sg, tri, y_ret, w_out_bf16, x, gate.reshape(b, 1, d), final_g.reshape(1, d))


def kernel(x, c, norm_g, w_ada, b_ada, w_in, w_out, final_g):
    b, s, d = x.shape
    assert d == D_MODEL and s % max(SB_TQ, PROJ_TM) == 0
    c_pad = jnp.zeros((ADA_ROWS, d), F32).at[:b].set(c.astype(F32))
    mod = _ada(c_pad, w_ada, b_ada)[:, :b]
    tables = _retention_tables(s)
    tri = jnp.asarray(np.tril(np.ones((SB_TK, SB_TK))), dtype=BF16)
    w_in_b = w_in.astype(BF16)
    w_out_b = w_out.astype(BF16)

    h = x.astype(F32)
    for layer in range(DEPTH):
        shift, scale, gate = jnp.split(mod[layer], 3, axis=-1)
        y_ret, sb, sg = _inproj(h, norm_g[layer], scale, shift, w_in_b[layer], tables)
        h = _stick_breaking_outproj(sb, sg, tri, y_ret, w_out_b[layer], h, gate, final_g,
                                    layer == DEPTH - 1)
    return h.astype(x.dtype)
```

```python
import functools

import jax
import jax.numpy as jnp
import numpy as np
from jax import lax
from jax.experimental import pallas as pl
from jax.experimental.pallas import tpu as pltpu

D_MODEL = 1024
DEPTH = 2
CHUNK = 64
D_RET = 512
D_SB = 512
RET_HEADS = 4
RET_HEAD_DIM = 128
SB_HEADS = 8
SB_HEAD_DIM = 64
ROPE_BASE = 10000.0
EPS = 1e-6
LOG2E = 1.4426950408889634
LOG2E_HI = 1.4453125
LOG2E_LO = LOG2E - LOG2E_HI

LANES = 128
SB_PAIRS = SB_HEADS * SB_HEAD_DIM // LANES
MIB = 1024 * 1024

ADA_ROWS = 8
ADA_TN = 1536
PROJ_TN = 512
RET_L = 256
PROJ_TM = 2 * RET_L
SB_TK = 256
SB_SUB = 2
SB_TQ = SB_SUB * SB_TK

SB_EXP2_CLAMP = 126.0
SB_DEAD = 151.0
SB_MASKED = -1e30
SB_Q_SCALE = LOG2E * SB_HEAD_DIM ** -0.5

F32 = jnp.float32
BF16 = jnp.bfloat16


def _silu(x):
    return x * (1.0 / (1.0 + jnp.exp(-x)))


def _ada_kernel(c_ref, w_ref, b_ref, o_ref):
    c_act = _silu(c_ref[...])
    o_ref[0] = jnp.dot(c_act.astype(BF16), w_ref[0].astype(BF16),
                       preferred_element_type=F32) + b_ref[0]


def _ada(c_pad, w_ada, b_ada):
    n = 3 * D_MODEL
    return pl.pallas_call(
        _ada_kernel,
        grid=(DEPTH, n // ADA_TN),
        in_specs=[
            pl.BlockSpec((ADA_ROWS, D_MODEL), lambda l, j: (0, 0)),
            pl.BlockSpec((1, D_MODEL, ADA_TN), lambda l, j: (l, 0, j)),
            pl.BlockSpec((1, 1, ADA_TN), lambda l, j: (l, 0, j)),
        ],
        out_specs=pl.BlockSpec((1, ADA_ROWS, ADA_TN), lambda l, j: (l, 0, j)),
        out_shape=jax.ShapeDtypeStruct((DEPTH, ADA_ROWS, n), F32),
        name="ada_mod",
    )(c_pad, w_ada, b_ada.reshape(DEPTH, 1, n))


def _retention_tables(seq):
    half = RET_HEAD_DIM // 2
    pos = np.arange(seq, dtype=np.float64)
    inv = ROPE_BASE ** (-np.arange(half, dtype=np.float64) / half)
    ang = pos[:, None] * inv[None, :]
    cos, sin = np.cos(ang), np.sin(ang)
    cos2 = np.concatenate([cos, cos], axis=-1)
    sin2 = np.concatenate([-sin, sin], axis=-1)

    log_gamma = np.log1p(-(2.0 ** (-5.0 - np.arange(RET_HEADS, dtype=np.float64))))
    idx = np.arange(RET_L, dtype=np.float64)
    diff = idx[:, None] - idx[None, :]
    same = (np.arange(RET_L)[:, None] // CHUNK) == (np.arange(RET_L)[None, :] // CHUNK)
    expo = np.where(same, np.abs(diff), diff)
    dmask = np.where(same | (diff > 0), np.exp(expo[None] * log_gamma[:, None, None]), 0.0)
    q_dec = np.exp((idx + 1.0)[None, :] * log_gamma[:, None])
    k_dec = np.exp((RET_L - 1.0 - idx)[None, :] * log_gamma[:, None])
    blk_dec = np.exp(RET_L * log_gamma)
    rep = lambda t: np.broadcast_to(t[:, :, None], (RET_HEADS, RET_L, LANES))
    blk = np.broadcast_to(blk_dec[:, None, None], (RET_HEADS, 1, LANES))
    return tuple(jnp.asarray(t, dtype=F32) for t in (cos2, sin2, dmask, rep(q_dec), rep(k_dec), blk))


def _retention_head(h, rq, rk, rv, rg, cos, sin, dm_ref, qd_ref, kd_ref, bd_ref, state_ref):
    half = RET_HEAD_DIM // 2
    cols = slice(h * RET_HEAD_DIM, (h + 1) * RET_HEAD_DIM)
    q = rq[:, cols]
    k = rk[:, cols]
    q = q * cos + pltpu.roll(q, half, axis=1) * sin
    k = (k * cos + pltpu.roll(k, half, axis=1) * sin) * (RET_HEAD_DIM ** -0.5)
    vb = rv[:, cols].astype(BF16)
    qb = q.astype(BF16)
    scores = lax.dot_general(qb, k.astype(BF16), (((1,), (1,)), ((), ())),
                             preferred_element_type=F32) * dm_ref[h]
    o = jnp.dot(scores.astype(BF16), vb, preferred_element_type=F32)
    state = state_ref[h]
    o = o + jnp.dot(qb, state.astype(BF16), preferred_element_type=F32) * qd_ref[h]
    kv = lax.dot_general((k * kd_ref[h]).astype(BF16), vb, (((0,), (0,)), ((), ())),
                         preferred_element_type=F32)
    state_ref[h] = state * bd_ref[h] + kv
    mu = jnp.mean(o, axis=-1, keepdims=True)
    oc = o - mu
    var = jnp.mean(oc * oc, axis=-1, keepdims=True)
    return oc * lax.rsqrt(var + EPS) * _silu(rg[:, cols])


def _inproj_kernel(x_ref, g_ref, shift_ref, scale_ref, w32_ref, cos_ref, sin_ref, dm_ref, qd_ref,
                   kd_ref, bd_ref, yret_ref, sb_ref, sg_ref, state_ref, w_ref):
    bi = pl.program_id(0)

    @pl.when(jnp.logical_and(bi == 0, pl.program_id(1) == 0))
    def _():
        w_ref[...] = w32_ref[0].astype(BF16)

    @pl.when(pl.program_id(1) == 0)
    def _():
        state_ref[...] = jnp.zeros_like(state_ref)

    x = x_ref[0]
    inv = lax.rsqrt(jnp.mean(x * x, axis=-1, keepdims=True) + EPS)
    scale = scale_ref[0, pl.ds(bi, 1), :]
    shift = shift_ref[0, pl.ds(bi, 1), :]
    h = (x * inv * g_ref[0]) * (1.0 + scale) + shift
    hb = h.astype(BF16)
    proj = lambda n: jnp.dot(hb, w_ref[:, n * PROJ_TN:(n + 1) * PROJ_TN], preferred_element_type=F32)
    n_ret = 4 * D_RET // PROJ_TN
    n_qkv = 3 * D_SB // PROJ_TN
    ret = [proj(n) for n in range(n_ret)]
    for blk in range(PROJ_TM // RET_L):
        rows = slice(blk * RET_L, (blk + 1) * RET_L)
        rq, rk, rv, rg = [r[rows] for r in ret]
        cos = cos_ref[rows, :]
        sin = sin_ref[rows, :]
        for hd in range(RET_HEADS):
            y = _retention_head(hd, rq, rk, rv, rg, cos, sin, dm_ref, qd_ref, kd_ref, bd_ref,
                                state_ref)
            yret_ref[0, rows, hd * RET_HEAD_DIM:(hd + 1) * RET_HEAD_DIM] = y.astype(yret_ref.dtype)
    for n in range(n_qkv):
        p = proj(n_ret + n)
        if n * PROJ_TN < D_SB:
            p = p * SB_Q_SCALE
        sb_ref[0, :, n * PROJ_TN:(n + 1) * PROJ_TN] = p.astype(BF16)
    sg_ref[0] = proj(n_ret + n_qkv)


def _inproj(x, norm_g, mod, w_in, tables, layer):
    b, s, d = x.shape
    d_in = w_in.shape[2]
    cos2, sin2, dmask, q_dec, k_dec, blk = tables
    once = lambda a: pl.BlockSpec(a.shape, lambda bi, i: (0,) * a.ndim, pipeline_mode=pl.Buffered(1))
    of_layer = lambda shape, col=0: pl.BlockSpec((1,) + shape, lambda bi, i: (layer, 0, col),
                                                 pipeline_mode=pl.Buffered(1))
    return pl.pallas_call(
        _inproj_kernel,
        grid=(b, s // PROJ_TM),
        in_specs=[
            pl.BlockSpec((1, PROJ_TM, d), lambda bi, i: (bi, i, 0)),
            of_layer((1, d)),
            of_layer((ADA_ROWS, d), 0), of_layer((ADA_ROWS, d), 1),
            of_layer((d, d_in)),
            pl.BlockSpec((PROJ_TM, LANES), lambda bi, i: (i, 0)),
            pl.BlockSpec((PROJ_TM, LANES), lambda bi, i: (i, 0)),
            once(dmask), once(q_dec), once(k_dec), once(blk),
        ],
        out_specs=[
            pl.BlockSpec((1, PROJ_TM, D_RET), lambda bi, i: (bi, i, 0)),
            pl.BlockSpec((1, PROJ_TM, 3 * D_SB), lambda bi, i: (bi, i, 0)),
            pl.BlockSpec((1, PROJ_TM, D_SB), lambda bi, i: (bi, i, 0)),
        ],
        out_shape=[
            jax.ShapeDtypeStruct((b, s, D_RET), BF16),
            jax.ShapeDtypeStruct((b, s, 3 * D_SB), BF16),
            jax.ShapeDtypeStruct((b, s, D_SB), F32),
        ],
        scratch_shapes=[pltpu.VMEM((RET_HEADS, RET_HEAD_DIM, RET_HEAD_DIM), F32),
                        pltpu.VMEM((d, d_in), BF16)],
        compiler_params=pltpu.CompilerParams(vmem_limit_bytes=56 * MIB),
        name="in_proj_retention",
    )(x, norm_g.reshape(DEPTH, 1, d), mod, mod, w_in, cos2, sin2, dmask, q_dec, k_dec, blk)


def _softplus2(u):
    ln = jnp.log(1.0 + jnp.exp2(jnp.minimum(u, SB_EXP2_CLAMP)))
    return jnp.maximum(u, ln * LOG2E_HI + ln * LOG2E_LO)


def _sb_scores(qm_ref, k_ref, u_ref, sp_ref, sub, slot, key0, diagonal, bias=None):
    for pair in range(SB_PAIRS):
        kb = k_ref[0, pl.ds(key0, SB_TK), pair * LANES:(pair + 1) * LANES]
        qm = qm_ref[sub, pair].reshape(2 * SB_TK, LANES)
        u = lax.dot_general(qm, kb, (((1,), (1,)), ((), ())), preferred_element_type=F32)
        if bias is not None:
            u = u + bias
        sp = _softplus2(u.astype(BF16))
        if diagonal:
            t_loc = lax.broadcasted_iota(jnp.int32, (SB_TK, SB_TK), 0)
            s_loc = lax.broadcasted_iota(jnp.int32, (SB_TK, SB_TK), 1)
            m = s_loc < t_loc
            mask = jnp.concatenate([m, m], axis=0)
            sp = jnp.where(mask, sp, jnp.zeros_like(sp))
            u = jnp.where(mask, u, SB_MASKED)
        u_ref[slot, pair] = u.reshape(2, SB_TK, SB_TK)
        sp_ref[slot, pair] = sp.reshape(2, SB_TK, SB_TK)


def _sb_values(v_ref, tri_ref, u_ref, sp_ref, acc_ref, ncarry_ref, sub, slot, key0):
    for pair in range(SB_PAIRS):
        vb = v_ref[0, pl.ds(key0, SB_TK), pair * LANES:(pair + 1) * LANES]
        sp = sp_ref[slot, pair].reshape(2 * SB_TK, SB_TK)
        r = jnp.dot(sp, tri_ref[...], preferred_element_type=F32)
        u = u_ref[slot, pair].reshape(2 * SB_TK, SB_TK)
        ncarry = ncarry_ref[sub, pair].reshape(2 * SB_TK, LANES)
        a = jnp.exp2(u - r + jnp.concatenate([ncarry] * (SB_TK // LANES), axis=1))
        pv = jnp.dot(a.astype(BF16), vb, preferred_element_type=F32)
        acc_ref[sub, pair] += pv.reshape(2, SB_TK, LANES)
        ncarry_ref[sub, pair] = (ncarry - r[:, :1]).reshape(2, SB_TK, LANES)


def _sb_kernel(q_ref, k_ref, v_ref, g_ref, tri_ref, yr_ref, w32_ref, x_ref, gate_ref, fg_ref, o_ref,
               qm_ref, acc_ref, ncarry_ref, u_ref, sp_ref, ys_ref, res_ref, w_ref, *, final):
    step = pl.program_id(1)
    bi = pl.program_id(0)

    @pl.when(jnp.logical_and(bi == 0, step == 0))
    def _():
        w_ref[...] = w32_ref[0].astype(BF16)

    res_ref[...] = jnp.dot(yr_ref[0], w_ref[:D_RET, :], preferred_element_type=F32)
    lane = lax.broadcasted_iota(jnp.int32, (SB_TK, LANES), 1)
    first = lane < SB_HEAD_DIM
    for sub in range(SB_SUB):
        for pair in range(SB_PAIRS):
            q = q_ref[0, sub * SB_TK:(sub + 1) * SB_TK, pair * LANES:(pair + 1) * LANES]
            zero = jnp.zeros_like(q)
            qm_ref[sub, pair, 0] = jnp.where(first, q, zero)
            qm_ref[sub, pair, 1] = jnp.where(first, zero, q)
    acc_ref[...] = jnp.zeros_like(acc_ref)
    ncarry_ref[...] = jnp.zeros_like(ncarry_ref)

    scores = functools.partial(_sb_scores, qm_ref, k_ref, u_ref, sp_ref)
    values = functools.partial(_sb_values, v_ref, tri_ref, u_ref, sp_ref, acc_ref, ncarry_ref)
    key_of = lambda blk: pl.multiple_of(blk * SB_TK, SB_TK)

    own = lambda sub: step * SB_SUB + sub
    no_prev = jnp.where(step == 0, SB_MASKED, 0.0)
    sequence = []
    for sub in range(SB_SUB):
        sequence.append((sub, own(sub), True, None))
        prev = jnp.maximum(own(sub) - 1, 0) if sub == 0 else own(sub) - 1
        sequence.append((sub, prev, False, no_prev if sub == 0 else None))
    for n, (sub, blk, diagonal, bias) in enumerate(sequence):
        scores(sub, n % 2, key_of(blk), diagonal, bias)
        if n > 0:
            p_sub, p_blk, _, _ = sequence[n - 1]
            values(p_sub, (n - 1) % 2, key_of(p_blk))
    p_sub, p_blk, _, _ = sequence[-1]
    values(p_sub, (len(sequence) - 1) % 2, key_of(p_blk))

    live = lambda sub: (jnp.max(ncarry_ref[sub]) > -SB_DEAD).astype(jnp.int32)
    live_now = [live(sub) for sub in range(SB_SUB)]
    for sub in range(SB_SUB):
        def body(state):
            blk, _ = state
            scores(sub, 0, key_of(blk), False)
            values(sub, 0, key_of(blk))
            return blk - 1, live(sub)

        lax.while_loop(lambda state: jnp.logical_and(state[0] >= 0, state[1] > 0),
                       body, (own(sub) - 2, live_now[sub]))

    for sub in range(SB_SUB):
        rows = slice(sub * SB_TK, (sub + 1) * SB_TK)
        for pair in range(SB_PAIRS):
            lanes = slice(pair * LANES, (pair + 1) * LANES)
            y = jnp.where(first, acc_ref[sub, pair, 0], acc_ref[sub, pair, 1])
            ys_ref[rows, lanes] = (y * _silu(g_ref[0, rows, lanes])).astype(ys_ref.dtype)

    y = res_ref[...] + jnp.dot(ys_ref[...], w_ref[D_RET:, :], preferred_element_type=F32)
    xn = x_ref[0] + gate_ref[0, pl.ds(bi, 1), :] * y
    if final:
        inv = lax.rsqrt(jnp.mean(xn * xn, axis=-1, keepdims=True) + EPS)
        xn = xn * inv * fg_ref[...]
    o_ref[0] = xn.astype(o_ref.dtype)


def _stick_breaking_outproj(sb, sg, tri, y_ret, w_out, x, mod, final_g, layer):
    b, s, d = x.shape
    final = layer == DEPTH - 1
    rows = lambda width: pl.BlockSpec((1, SB_TQ, width), lambda bi, i: (bi, i, 0))
    once = lambda a: pl.BlockSpec(a.shape, lambda bi, i: (0,) * a.ndim, pipeline_mode=pl.Buffered(1))
    resident = lambda col: pl.BlockSpec((1, s, D_SB), lambda bi, i: (bi, 0, col),
                                        pipeline_mode=pl.Buffered(1))
    return pl.pallas_call(
        functools.partial(_sb_kernel, final=final),
        grid=(b, s // SB_TQ),
        in_specs=[
            rows(D_SB), resident(1), resident(2), rows(D_SB), once(tri),
            rows(D_RET),
            pl.BlockSpec((1, d, d), lambda bi, i: (layer, 0, 0), pipeline_mode=pl.Buffered(1)),
            rows(d),
            pl.BlockSpec((1, ADA_ROWS, d), lambda bi, i: (layer, 0, 2),
                         pipeline_mode=pl.Buffered(1)),
            pl.BlockSpec((1, d), lambda bi, i: (0, 0)),
        ],
        out_specs=rows(d),
        out_shape=jax.ShapeDtypeStruct((b, s, d), x.dtype),
        scratch_shapes=[
            pltpu.VMEM((SB_SUB, SB_PAIRS, 2, SB_TK, LANES), BF16),
            pltpu.VMEM((SB_SUB, SB_PAIRS, 2, SB_TK, LANES), F32),
            pltpu.VMEM((SB_SUB, SB_PAIRS, 2, SB_TK, LANES), F32),
            pltpu.VMEM((2, SB_PAIRS, 2, SB_TK, SB_TK), F32),
            pltpu.VMEM((2, SB_PAIRS, 2, SB_TK, SB_TK), BF16),
            pltpu.VMEM((SB_TQ, D_SB), BF16),
            pltpu.VMEM((SB_TQ, D_MODEL), F32),
            pltpu.VMEM((D_MODEL, D_MODEL), BF16),
        ],
        compiler_params=pltpu.CompilerParams(vmem_limit_bytes=56 * MIB),
        name="stick_breaking_out_proj_final" if final else "stick_breaking_out_proj",
    )(sb, sb, sb, sg, tri, y_ret, w_out, x, mod, final_g.reshape(1, d))


def kernel(x, c, norm_g, w_ada, b_ada, w_in, w_out, final_g):
    b, s, d = x.shape
    assert d == D_MODEL and s % max(SB_TQ, PROJ_TM) == 0
    c_pad = jnp.zeros((ADA_ROWS, d), F32).at[:b].set(c.astype(F32))
    mod = _ada(c_pad, w_ada, b_ada)
    tables = _retention_tables(s)
    tri = jnp.asarray(np.tril(np.ones((SB_TK, SB_TK))), dtype=BF16)

    h = x.astype(F32)
    for layer in range(DEPTH):
        y_ret, sb, sg = _inproj(h, norm_g, mod, w_in, tables, layer)
        h = _stick_breaking_outproj(sb, sg, tri, y_ret, w_out, h, mod, final_g, layer)
    return h.astype(x.dtype)
```

```python
import functools

import jax
import jax.numpy as jnp
import numpy as np
from jax import lax
from jax.experimental import pallas as pl
from jax.experimental.pallas import tpu as pltpu

D_MODEL = 1024
DEPTH = 2
CHUNK = 64
D_RET = 512
D_SB = 512
RET_HEADS = 4
RET_HEAD_DIM = 128
SB_HEADS = 8
SB_HEAD_DIM = 64
ROPE_BASE = 10000.0
EPS = 1e-6
LOG2E = 1.4426950408889634
LOG2E_HI = 1.4453125
LOG2E_LO = LOG2E - LOG2E_HI

LANES = 128
SB_PAIRS = SB_HEADS * SB_HEAD_DIM // LANES
MIB = 1024 * 1024

ADA_ROWS = 8
ADA_TN = 1536
PROJ_TN = 512
RET_L = 256
PROJ_TM = 2 * RET_L
SB_TK = 256
SB_SUB = 2
SB_TQ = SB_SUB * SB_TK
SB_EAGER = 192

SB_EXP2_CLAMP = 126.0
SB_DEAD = 151.0
SB_MASKED = -1e30
SB_Q_SCALE = LOG2E * SB_HEAD_DIM ** -0.5

F32 = jnp.float32
BF16 = jnp.bfloat16


def _silu(x):
    return x * (1.0 / (1.0 + jnp.exp(-x)))


def _ada_kernel(c_ref, w_ref, b_ref, o_ref):
    c_act = _silu(c_ref[...])
    o_ref[0] = jnp.dot(c_act.astype(BF16), w_ref[0].astype(BF16),
                       preferred_element_type=F32) + b_ref[0]


def _ada(c_pad, w_ada, b_ada):
    n = 3 * D_MODEL
    return pl.pallas_call(
        _ada_kernel,
        grid=(DEPTH, n // ADA_TN),
        in_specs=[
            pl.BlockSpec((ADA_ROWS, D_MODEL), lambda l, j: (0, 0)),
            pl.BlockSpec((1, D_MODEL, ADA_TN), lambda l, j: (l, 0, j)),
            pl.BlockSpec((1, 1, ADA_TN), lambda l, j: (l, 0, j)),
        ],
        out_specs=pl.BlockSpec((1, ADA_ROWS, ADA_TN), lambda l, j: (l, 0, j)),
        out_shape=jax.ShapeDtypeStruct((DEPTH, ADA_ROWS, n), F32),
        name="ada_mod",
    )(c_pad, w_ada, b_ada.reshape(DEPTH, 1, n))


def _retention_tables(seq):
    half = RET_HEAD_DIM // 2
    pos = np.arange(seq, dtype=np.float64)
    inv = ROPE_BASE ** (-np.arange(half, dtype=np.float64) / half)
    ang = pos[:, None] * inv[None, :]
    cos, sin = np.cos(ang), np.sin(ang)
    cos2 = np.concatenate([cos, cos], axis=-1)
    sin2 = np.concatenate([-sin, sin], axis=-1)

    log_gamma = np.log1p(-(2.0 ** (-5.0 - np.arange(RET_HEADS, dtype=np.float64))))
    idx = np.arange(RET_L, dtype=np.float64)
    diff = idx[:, None] - idx[None, :]
    same = (np.arange(RET_L)[:, None] // CHUNK) == (np.arange(RET_L)[None, :] // CHUNK)
    expo = np.where(same, np.abs(diff), diff)
    dmask = np.where(same | (diff > 0), np.exp(expo[None] * log_gamma[:, None, None]), 0.0)
    q_dec = np.exp((idx + 1.0)[None, :] * log_gamma[:, None])
    k_dec = np.exp((RET_L - 1.0 - idx)[None, :] * log_gamma[:, None])
    blk_dec = np.exp(RET_L * log_gamma)
    rep = lambda t: np.broadcast_to(t[:, :, None], (RET_HEADS, RET_L, LANES))
    blk = np.broadcast_to(blk_dec[:, None, None], (RET_HEADS, 1, LANES))
    return tuple(jnp.asarray(t, dtype=F32) for t in (cos2, sin2, dmask, rep(q_dec), rep(k_dec), blk))


def _retention_head(h, rq, rk, rv, rg, cos, sin, dm_ref, qd_ref, kd_ref, bd_ref, state_ref):
    half = RET_HEAD_DIM // 2
    cols = slice(h * RET_HEAD_DIM, (h + 1) * RET_HEAD_DIM)
    q = rq[:, cols]
    k = rk[:, cols]
    q = q * cos + pltpu.roll(q, half, axis=1) * sin
    k = (k * cos + pltpu.roll(k, half, axis=1) * sin) * (RET_HEAD_DIM ** -0.5)
    vb = rv[:, cols].astype(BF16)
    qb = q.astype(BF16)
    scores = lax.dot_general(qb, k.astype(BF16), (((1,), (1,)), ((), ())),
                             preferred_element_type=F32) * dm_ref[h]
    o = jnp.dot(scores.astype(BF16), vb, preferred_element_type=F32)
    state = state_ref[h]
    o = o + jnp.dot(qb, state.astype(BF16), preferred_element_type=F32) * qd_ref[h]
    kv = lax.dot_general((k * kd_ref[h]).astype(BF16), vb, (((0,), (0,)), ((), ())),
                         preferred_element_type=F32)
    state_ref[h] = state * bd_ref[h] + kv
    mu = jnp.mean(o, axis=-1, keepdims=True)
    oc = o - mu
    var = jnp.mean(oc * oc, axis=-1, keepdims=True)
    return oc * lax.rsqrt(var + EPS) * _silu(rg[:, cols])


def _inproj_kernel(x_ref, g_ref, shift_ref, scale_ref, w32_ref, cos_ref, sin_ref, dm_ref, qd_ref,
                   kd_ref, bd_ref, yret_ref, sb_ref, sg_ref, state_ref, w_ref):
    bi = pl.program_id(0)

    @pl.when(jnp.logical_and(bi == 0, pl.program_id(1) == 0))
    def _():
        w_ref[...] = w32_ref[0].astype(BF16)

    @pl.when(pl.program_id(1) == 0)
    def _():
        state_ref[...] = jnp.zeros_like(state_ref)

    x = x_ref[0]
    inv = lax.rsqrt(jnp.mean(x * x, axis=-1, keepdims=True) + EPS)
    scale = scale_ref[0, pl.ds(bi, 1), :]
    shift = shift_ref[0, pl.ds(bi, 1), :]
    h = (x * inv * g_ref[0]) * (1.0 + scale) + shift
    hb = h.astype(BF16)
    proj = lambda n: jnp.dot(hb, w_ref[:, n * PROJ_TN:(n + 1) * PROJ_TN], preferred_element_type=F32)
    n_ret = 4 * D_RET // PROJ_TN
    n_qkv = 3 * D_SB // PROJ_TN
    ret = [proj(n) for n in range(n_ret)]
    for blk in range(PROJ_TM // RET_L):
        rows = slice(blk * RET_L, (blk + 1) * RET_L)
        rq, rk, rv, rg = [r[rows] for r in ret]
        cos = cos_ref[rows, :]
        sin = sin_ref[rows, :]
        for hd in range(RET_HEADS):
            y = _retention_head(hd, rq, rk, rv, rg, cos, sin, dm_ref, qd_ref, kd_ref, bd_ref,
                                state_ref)
            yret_ref[0, rows, hd * RET_HEAD_DIM:(hd + 1) * RET_HEAD_DIM] = y.astype(yret_ref.dtype)
    for n in range(n_qkv):
        p = proj(n_ret + n)
        if n * PROJ_TN < D_SB:
            p = p * SB_Q_SCALE
        sb_ref[0, :, n * PROJ_TN:(n + 1) * PROJ_TN] = p.astype(BF16)
    sg_ref[0] = proj(n_ret + n_qkv)


def _inproj(x, norm_g, mod, w_in, tables, layer):
    b, s, d = x.shape
    d_in = w_in.shape[2]
    cos2, sin2, dmask, q_dec, k_dec, blk = tables
    once = lambda a: pl.BlockSpec(a.shape, lambda bi, i: (0,) * a.ndim, pipeline_mode=pl.Buffered(1))
    of_layer = lambda shape, col=0: pl.BlockSpec((1,) + shape, lambda bi, i: (layer, 0, col),
                                                 pipeline_mode=pl.Buffered(1))
    return pl.pallas_call(
        _inproj_kernel,
        grid=(b, s // PROJ_TM),
        in_specs=[
            pl.BlockSpec((1, PROJ_TM, d), lambda bi, i: (bi, i, 0)),
            of_layer((1, d)),
            of_layer((ADA_ROWS, d), 0), of_layer((ADA_ROWS, d), 1),
            of_layer((d, d_in)),
            pl.BlockSpec((PROJ_TM, LANES), lambda bi, i: (i, 0)),
            pl.BlockSpec((PROJ_TM, LANES), lambda bi, i: (i, 0)),
            once(dmask), once(q_dec), once(k_dec), once(blk),
        ],
        out_specs=[
            pl.BlockSpec((1, PROJ_TM, D_RET), lambda bi, i: (bi, i, 0)),
            pl.BlockSpec((1, PROJ_TM, 3 * D_SB), lambda bi, i: (bi, i, 0)),
            pl.BlockSpec((1, PROJ_TM, D_SB), lambda bi, i: (bi, i, 0)),
        ],
        out_shape=[
            jax.ShapeDtypeStruct((b, s, D_RET), BF16),
            jax.ShapeDtypeStruct((b, s, 3 * D_SB), BF16),
            jax.ShapeDtypeStruct((b, s, D_SB), F32),
        ],
        scratch_shapes=[pltpu.VMEM((RET_HEADS, RET_HEAD_DIM, RET_HEAD_DIM), F32),
                        pltpu.VMEM((d, d_in), BF16)],
        compiler_params=pltpu.CompilerParams(dimension_semantics=("arbitrary", "arbitrary"),
                                             vmem_limit_bytes=56 * MIB),
        name="in_proj_retention",
    )(x, norm_g.reshape(DEPTH, 1, d), mod, mod, w_in, cos2, sin2, dmask, q_dec, k_dec, blk)


def _softplus2(u):
    ln = jnp.log(1.0 + jnp.exp2(jnp.minimum(u, SB_EXP2_CLAMP)))
    return jnp.maximum(u, ln * LOG2E_HI + ln * LOG2E_LO)


def _sb_scores(qm_ref, k_ref, u_ref, sp_ref, sub, slot, key0, diagonal, bias=None,
               rows=(0, SB_TK)):
    r0, r1 = rows
    n = r1 - r0
    assert not diagonal or n == SB_TK
    for pair in range(SB_PAIRS):
        kb = k_ref[0, pl.ds(key0, SB_TK), pair * LANES:(pair + 1) * LANES]
        qm = qm_ref[sub, pair, :, r0:r1, :].reshape(2 * n, LANES)
        u = lax.dot_general(qm, kb, (((1,), (1,)), ((), ())), preferred_element_type=F32)
        if bias is not None:
            u = u + bias
        sp = _softplus2(u.astype(BF16))
        if diagonal:
            t_loc = lax.broadcasted_iota(jnp.int32, (SB_TK, SB_TK), 0)
            s_loc = lax.broadcasted_iota(jnp.int32, (SB_TK, SB_TK), 1)
            m = s_loc < t_loc
            mask = jnp.concatenate([m, m], axis=0)
            sp = jnp.where(mask, sp, jnp.zeros_like(sp))
            u = jnp.where(mask, u, SB_MASKED)
        u_ref[slot, pair, :, r0:r1, :] = u.reshape(2, n, SB_TK)
        sp_ref[slot, pair, :, r0:r1, :] = sp.reshape(2, n, SB_TK)


def _sb_values(v_ref, tri_ref, u_ref, sp_ref, acc_ref, ncarry_ref, sub, slot, key0,
               rows=(0, SB_TK)):
    r0, r1 = rows
    n = r1 - r0
    for pair in range(SB_PAIRS):
        vb = v_ref[0, pl.ds(key0, SB_TK), pair * LANES:(pair + 1) * LANES]
        sp = sp_ref[slot, pair, :, r0:r1, :].reshape(2 * n, SB_TK)
        r = jnp.dot(sp, tri_ref[...], preferred_element_type=F32)
        u = u_ref[slot, pair, :, r0:r1, :].reshape(2 * n, SB_TK)
        ncarry = ncarry_ref[sub, pair, :, r0:r1, :].reshape(2 * n, LANES)
        a = jnp.exp2(u - r + jnp.concatenate([ncarry] * (SB_TK // LANES), axis=1))
        pv = jnp.dot(a.astype(BF16), vb, preferred_element_type=F32)
        acc_ref[sub, pair, :, r0:r1, :] += pv.reshape(2, n, LANES)
        ncarry_ref[sub, pair, :, r0:r1, :] = (ncarry - r[:, :1]).reshape(2, n, LANES)


def _sb_kernel(q_ref, k_ref, v_ref, g_ref, tri_ref, yr_ref, w32_ref, x_ref, gate_ref, fg_ref, o_ref,
               qm_ref, acc_ref, ncarry_ref, u_ref, sp_ref, ys_ref, res_ref, w_ref, *, final):
    step = pl.program_id(1)
    bi = pl.program_id(0)

    @pl.when(jnp.logical_and(bi == 0, step == 0))
    def _():
        w_ref[...] = w32_ref[0].astype(BF16)

    res_ref[...] = jnp.dot(yr_ref[0], w_ref[:D_RET, :], preferred_element_type=F32)
    lane = lax.broadcasted_iota(jnp.int32, (SB_TK, LANES), 1)
    first = lane < SB_HEAD_DIM
    for sub in range(SB_SUB):
        for pair in range(SB_PAIRS):
            q = q_ref[0, sub * SB_TK:(sub + 1) * SB_TK, pair * LANES:(pair + 1) * LANES]
            zero = jnp.zeros_like(q)
            qm_ref[sub, pair, 0] = jnp.where(first, q, zero)
            qm_ref[sub, pair, 1] = jnp.where(first, zero, q)
    acc_ref[...] = jnp.zeros_like(acc_ref)
    ncarry_ref[...] = jnp.zeros_like(ncarry_ref)

    scores = functools.partial(_sb_scores, qm_ref, k_ref, u_ref, sp_ref)
    values = functools.partial(_sb_values, v_ref, tri_ref, u_ref, sp_ref, acc_ref, ncarry_ref)
    key_of = lambda blk: pl.multiple_of(blk * SB_TK, SB_TK)

    own = lambda sub: step * SB_SUB + sub
    prev_of = lambda sub: jnp.maximum(own(sub) - 1, 0) if sub == 0 else own(sub) - 1
    prev_bias = lambda sub: jnp.where(step == 0, SB_MASKED, 0.0) if sub == 0 else None
    eager, late = (0, SB_EAGER), (SB_EAGER, SB_TK)
    sequence = []
    for sub in range(SB_SUB):
        sequence.append((sub, own(sub), True, None, (0, SB_TK)))
        sequence.append((sub, prev_of(sub), False, prev_bias(sub), eager))
    for n, (sub, blk, diagonal, bias, rows) in enumerate(sequence):
        scores(sub, n % 2, key_of(blk), diagonal, bias, rows)
        if n > 0:
            p_sub, p_blk, _, _, p_rows = sequence[n - 1]
            values(p_sub, (n - 1) % 2, key_of(p_blk), p_rows)
    p_sub, p_blk, _, _, p_rows = sequence[-1]
    values(p_sub, (len(sequence) - 1) % 2, key_of(p_blk), p_rows)

    live = lambda sub, r0=0: (jnp.max(ncarry_ref[sub, :, :, r0:, :]) > -SB_DEAD).astype(jnp.int32)
    live_late = [live(sub, SB_EAGER) for sub in range(SB_SUB)]
    live_now = [live(sub) for sub in range(SB_SUB)]
    for sub in range(SB_SUB):
        @pl.when(live_late[sub] > 0)
        def _():
            scores(sub, 0, key_of(prev_of(sub)), False, prev_bias(sub), late)
            values(sub, 0, key_of(prev_of(sub)), late)

        def body(state):
            blk, _ = state
            scores(sub, 0, key_of(blk), False)
            values(sub, 0, key_of(blk))
            return blk - 1, live(sub)

        lax.while_loop(lambda state: jnp.logical_and(state[0] >= 0, state[1] > 0),
                       body, (own(sub) - 2, live_now[sub]))

    for sub in range(SB_SUB):
        rows = slice(sub * SB_TK, (sub + 1) * SB_TK)
        for pair in range(SB_PAIRS):
            lanes = slice(pair * LANES, (pair + 1) * LANES)
            y = jnp.where(first, acc_ref[sub, pair, 0], acc_ref[sub, pair, 1])
            ys_ref[rows, lanes] = (y * _silu(g_ref[0, rows, lanes])).astype(ys_ref.dtype)

    y = res_ref[...] + jnp.dot(ys_ref[...], w_ref[D_RET:, :], preferred_element_type=F32)
    xn = x_ref[0] + gate_ref[0, pl.ds(bi, 1), :] * y
    if final:
        inv = lax.rsqrt(jnp.mean(xn * xn, axis=-1, keepdims=True) + EPS)
        xn = xn * inv * fg_ref[...]
    o_ref[0] = xn.astype(o_ref.dtype)


def _stick_breaking_outproj(sb, sg, tri, y_ret, w_out, x, mod, final_g, layer):
    b, s, d = x.shape
    final = layer == DEPTH - 1
    rows = lambda width: pl.BlockSpec((1, SB_TQ, width), lambda bi, i: (bi, i, 0))
    once = lambda a: pl.BlockSpec(a.shape, lambda bi, i: (0,) * a.ndim, pipeline_mode=pl.Buffered(1))
    resident = lambda col: pl.BlockSpec((1, s, D_SB), lambda bi, i: (bi, 0, col),
                                        pipeline_mode=pl.Buffered(1))
    return pl.pallas_call(
        functools.partial(_sb_kernel, final=final),
        grid=(b, s // SB_TQ),
        in_specs=[
            rows(D_SB), resident(1), resident(2), rows(D_SB), once(tri),
            rows(D_RET),
            pl.BlockSpec((1, d, d), lambda bi, i: (layer, 0, 0), pipeline_mode=pl.Buffered(1)),
            rows(d),
            pl.BlockSpec((1, ADA_ROWS, d), lambda bi, i: (layer, 0, 2),
                         pipeline_mode=pl.Buffered(1)),
            pl.BlockSpec((1, d), lambda bi, i: (0, 0)),
        ],
        out_specs=rows(d),
        out_shape=jax.ShapeDtypeStruct((b, s, d), x.dtype),
        scratch_shapes=[
            pltpu.VMEM((SB_SUB, SB_PAIRS, 2, SB_TK, LANES), BF16),
            pltpu.VMEM((SB_SUB, SB_PAIRS, 2, SB_TK, LANES), F32),
            pltpu.VMEM((SB_SUB, SB_PAIRS, 2, SB_TK, LANES), F32),
            pltpu.VMEM((2, SB_PAIRS, 2, SB_TK, SB_TK), F32),
            pltpu.VMEM((2, SB_PAIRS, 2, SB_TK, SB_TK), BF16),
            pltpu.VMEM((SB_TQ, D_SB), BF16),
            pltpu.VMEM((SB_TQ, D_MODEL), F32),
            pltpu.VMEM((D_MODEL, D_MODEL), BF16),
        ],
        compiler_params=pltpu.CompilerParams(dimension_semantics=("arbitrary", "arbitrary"),
                                             vmem_limit_bytes=56 * MIB),
        name="stick_breaking_out_proj_final" if final else "stick_breaking_out_proj",
    )(sb, sb, sb, sg, tri, y_ret, w_out, x, mod, final_g.reshape(1, d))


def kernel(x, c, norm_g, w_ada, b_ada, w_in, w_out, final_g):
    b, s, d = x.shape
    assert d == D_MODEL and s % max(SB_TQ, PROJ_TM) == 0
    c_pad = jnp.zeros((ADA_ROWS, d), F32).at[:b].set(c.astype(F32))
    mod = _ada(c_pad, w_ada, b_ada)
    tables = _retention_tables(s)
    tri = jnp.asarray(np.tril(np.ones((SB_TK, SB_TK))), dtype=BF16)

    h = x.astype(F32)
    for layer in range(DEPTH):
        y_ret, sb, sg = _inproj(h, norm_g, mod, w_in, tables, layer)
        h = _stick_breaking_outproj(sb, sg, tri, y_ret, w_out, h, mod, final_g, layer)
    return h.astype(x.dtype)
```

```python
import functools

import jax
import jax.numpy as jnp
import numpy as np
from jax import lax
from jax.experimental import pallas as pl
from jax.experimental.pallas import tpu as pltpu

D_MODEL = 1024
DEPTH = 2
CHUNK = 64
D_RET = 512
D_SB = 512
RET_HEADS = 4
RET_HEAD_DIM = 128
SB_HEADS = 8
SB_HEAD_DIM = 64
ROPE_BASE = 10000.0
EPS = 1e-6
LOG2E = 1.4426950408889634
LOG2E_HI = 1.4453125
LOG2E_LO = LOG2E - LOG2E_HI

LANES = 128
SB_PAIRS = SB_HEADS * SB_HEAD_DIM // LANES
MIB = 1024 * 1024

ADA_ROWS = 8
ADA_TN = 1536
PROJ_TN = 512
RET_L = 256
PROJ_TM = 2 * RET_L
SB_TK = 256
SB_SUB = 2
SB_TQ = SB_SUB * SB_TK
SB_EAGER = 176

SB_EXP2_CLAMP = 126.0
SB_DEAD = 151.0
SB_MASKED = -1e30
SB_Q_SCALE = LOG2E * SB_HEAD_DIM ** -0.5

F32 = jnp.float32
BF16 = jnp.bfloat16


def _silu(x):
    return x * (1.0 / (1.0 + jnp.exp(-x)))


def _ada_kernel(c_ref, w_ref, b_ref, o_ref):
    c_act = _silu(c_ref[...])
    o_ref[0] = jnp.dot(c_act.astype(BF16), w_ref[0].astype(BF16),
                       preferred_element_type=F32) + b_ref[0]


def _ada(c_pad, w_ada, b_ada):
    n = 3 * D_MODEL
    return pl.pallas_call(
        _ada_kernel,
        grid=(DEPTH, n // ADA_TN),
        in_specs=[
            pl.BlockSpec((ADA_ROWS, D_MODEL), lambda l, j: (0, 0)),
            pl.BlockSpec((1, D_MODEL, ADA_TN), lambda l, j: (l, 0, j)),
            pl.BlockSpec((1, 1, ADA_TN), lambda l, j: (l, 0, j)),
        ],
        out_specs=pl.BlockSpec((1, ADA_ROWS, ADA_TN), lambda l, j: (l, 0, j)),
        out_shape=jax.ShapeDtypeStruct((DEPTH, ADA_ROWS, n), F32),
        name="ada_mod",
    )(c_pad, w_ada, b_ada.reshape(DEPTH, 1, n))


def _retention_tables(seq):
    half = RET_HEAD_DIM // 2
    pos = np.arange(seq, dtype=np.float64)
    inv = ROPE_BASE ** (-np.arange(half, dtype=np.float64) / half)
    ang = pos[:, None] * inv[None, :]
    cos, sin = np.cos(ang), np.sin(ang)
    cos2 = np.concatenate([cos, cos], axis=-1)
    sin2 = np.concatenate([-sin, sin], axis=-1)

    log_gamma = np.log1p(-(2.0 ** (-5.0 - np.arange(RET_HEADS, dtype=np.float64))))
    idx = np.arange(RET_L, dtype=np.float64)
    diff = idx[:, None] - idx[None, :]
    same = (np.arange(RET_L)[:, None] // CHUNK) == (np.arange(RET_L)[None, :] // CHUNK)
    expo = np.where(same, np.abs(diff), diff)
    dmask = np.where(same | (diff > 0), np.exp(expo[None] * log_gamma[:, None, None]), 0.0)
    q_dec = np.exp((idx + 1.0)[None, :] * log_gamma[:, None])
    k_dec = np.exp((RET_L - 1.0 - idx)[None, :] * log_gamma[:, None])
    blk_dec = np.exp(RET_L * log_gamma)
    rep = lambda t: np.broadcast_to(t[:, :, None], (RET_HEADS, RET_L, LANES))
    blk = np.broadcast_to(blk_dec[:, None, None], (RET_HEADS, 1, LANES))
    return tuple(jnp.asarray(t, dtype=F32) for t in (cos2, sin2, dmask, rep(q_dec), rep(k_dec), blk))


def _retention_head(h, rq, rk, rv, rg, cos, sin, dm_ref, qd_ref, kd_ref, bd_ref, state_ref):
    half = RET_HEAD_DIM // 2
    cols = slice(h * RET_HEAD_DIM, (h + 1) * RET_HEAD_DIM)
    q = rq[:, cols]
    k = rk[:, cols]
    q = q * cos + pltpu.roll(q, half, axis=1) * sin
    k = (k * cos + pltpu.roll(k, half, axis=1) * sin) * (RET_HEAD_DIM ** -0.5)
    vb = rv[:, cols].astype(BF16)
    qb = q.astype(BF16)
    scores = lax.dot_general(qb, k.astype(BF16), (((1,), (1,)), ((), ())),
                             preferred_element_type=F32) * dm_ref[h]
    o = jnp.dot(scores.astype(BF16), vb, preferred_element_type=F32)
    state = state_ref[h]
    o = o + jnp.dot(qb, state.astype(BF16), preferred_element_type=F32) * qd_ref[h]
    kv = lax.dot_general((k * kd_ref[h]).astype(BF16), vb, (((0,), (0,)), ((), ())),
                         preferred_element_type=F32)
    state_ref[h] = state * bd_ref[h] + kv
    mu = jnp.mean(o, axis=-1, keepdims=True)
    oc = o - mu
    var = jnp.mean(oc * oc, axis=-1, keepdims=True)
    return oc * lax.rsqrt(var + EPS) * _silu(rg[:, cols])


def _inproj_kernel(x_ref, g_ref, shift_ref, scale_ref, w32_ref, cos_ref, sin_ref, dm_ref, qd_ref,
                   kd_ref, bd_ref, yret_ref, sb_ref, sg_ref, state_ref, w_ref):
    bi = pl.program_id(0)

    @pl.when(jnp.logical_and(bi == 0, pl.program_id(1) == 0))
    def _():
        w_ref[...] = w32_ref[0].astype(BF16)

    @pl.when(pl.program_id(1) == 0)
    def _():
        state_ref[...] = jnp.zeros_like(state_ref)

    x = x_ref[0]
    inv = lax.rsqrt(jnp.mean(x * x, axis=-1, keepdims=True) + EPS)
    scale = scale_ref[0, pl.ds(bi, 1), :]
    shift = shift_ref[0, pl.ds(bi, 1), :]
    h = (x * inv * g_ref[0]) * (1.0 + scale) + shift
    hb = h.astype(BF16)
    proj = lambda n: jnp.dot(hb, w_ref[:, n * PROJ_TN:(n + 1) * PROJ_TN], preferred_element_type=F32)
    n_ret = 4 * D_RET // PROJ_TN
    n_qkv = 3 * D_SB // PROJ_TN
    ret = [proj(n) for n in range(n_ret)]
    for blk in range(PROJ_TM // RET_L):
        rows = slice(blk * RET_L, (blk + 1) * RET_L)
        rq, rk, rv, rg = [r[rows] for r in ret]
        cos = cos_ref[rows, :]
        sin = sin_ref[rows, :]
        for hd in range(RET_HEADS):
            y = _retention_head(hd, rq, rk, rv, rg, cos, sin, dm_ref, qd_ref, kd_ref, bd_ref,
                                state_ref)
            yret_ref[0, rows, hd * RET_HEAD_DIM:(hd + 1) * RET_HEAD_DIM] = y.astype(yret_ref.dtype)
    for n in range(n_qkv):
        p = proj(n_ret + n)
        if n * PROJ_TN < D_SB:
            p = p * SB_Q_SCALE
        sb_ref[0, :, n * PROJ_TN:(n + 1) * PROJ_TN] = p.astype(BF16)
    sg_ref[0] = proj(n_ret + n_qkv)


def _inproj(x, norm_g, mod, w_in, tables, layer):
    b, s, d = x.shape
    d_in = w_in.shape[2]
    cos2, sin2, dmask, q_dec, k_dec, blk = tables
    once = lambda a: pl.BlockSpec(a.shape, lambda bi, i: (0,) * a.ndim, pipeline_mode=pl.Buffered(1))
    of_layer = lambda shape, col=0: pl.BlockSpec((1,) + shape, lambda bi, i: (layer, 0, col),
                                                 pipeline_mode=pl.Buffered(1))
    return pl.pallas_call(
        _inproj_kernel,
        grid=(b, s // PROJ_TM),
        in_specs=[
            pl.BlockSpec((1, PROJ_TM, d), lambda bi, i: (bi, i, 0)),
            of_layer((1, d)),
            of_layer((ADA_ROWS, d), 0), of_layer((ADA_ROWS, d), 1),
            of_layer((d, d_in)),
            pl.BlockSpec((PROJ_TM, LANES), lambda bi, i: (i, 0)),
            pl.BlockSpec((PROJ_TM, LANES), lambda bi, i: (i, 0)),
            once(dmask), once(q_dec), once(k_dec), once(blk),
        ],
        out_specs=[
            pl.BlockSpec((1, PROJ_TM, D_RET), lambda bi, i: (bi, i, 0)),
            pl.BlockSpec((1, PROJ_TM, 3 * D_SB), lambda bi, i: (bi, i, 0)),
            pl.BlockSpec((1, PROJ_TM, D_SB), lambda bi, i: (bi, i, 0)),
        ],
        out_shape=[
            jax.ShapeDtypeStruct((b, s, D_RET), BF16),
            jax.ShapeDtypeStruct((b, s, 3 * D_SB), BF16),
            jax.ShapeDtypeStruct((b, s, D_SB), F32),
        ],
        scratch_shapes=[pltpu.VMEM((RET_HEADS, RET_HEAD_DIM, RET_HEAD_DIM), F32),
                        pltpu.VMEM((d, d_in), BF16)],
        compiler_params=pltpu.CompilerParams(dimension_semantics=("arbitrary", "arbitrary"),
                                             vmem_limit_bytes=56 * MIB),
        name="in_proj_retention",
    )(x, norm_g.reshape(DEPTH, 1, d), mod, mod, w_in, cos2, sin2, dmask, q_dec, k_dec, blk)


def _softplus2(u):
    ln = jnp.log(1.0 + jnp.exp2(jnp.minimum(u, SB_EXP2_CLAMP)))
    return jnp.maximum(u, ln * LOG2E_HI + ln * LOG2E_LO)


def _sb_scores(qm_ref, k_ref, u_ref, sp_ref, sub, slot, key0, diagonal, bias=None,
               rows=(0, SB_TK), nk=SB_TK):
    r0, r1 = rows
    n = r1 - r0
    assert not diagonal or nk >= r1
    for pair in range(SB_PAIRS):
        kb = k_ref[0, pl.ds(key0, nk), pair * LANES:(pair + 1) * LANES]
        qm = qm_ref[sub, pair, :, r0:r1, :].reshape(2 * n, LANES)
        u = lax.dot_general(qm, kb, (((1,), (1,)), ((), ())), preferred_element_type=F32)
        if bias is not None:
            u = u + bias
        sp = _softplus2(u.astype(BF16))
        if diagonal:
            t_loc = lax.broadcasted_iota(jnp.int32, (n, nk), 0) + r0
            s_loc = lax.broadcasted_iota(jnp.int32, (n, nk), 1)
            m = s_loc < t_loc
            mask = jnp.concatenate([m, m], axis=0)
            sp = jnp.where(mask, sp, jnp.zeros_like(sp))
            u = jnp.where(mask, u, SB_MASKED)
        u_ref[slot, pair, :, r0:r1, :nk] = u.reshape(2, n, nk)
        sp_ref[slot, pair, :, r0:r1, :nk] = sp.reshape(2, n, nk)


def _sb_values(v_ref, tri_ref, u_ref, sp_ref, acc_ref, ncarry_ref, sub, slot, key0,
               rows=(0, SB_TK), nk=SB_TK):
    r0, r1 = rows
    n = r1 - r0
    for pair in range(SB_PAIRS):
        vb = v_ref[0, pl.ds(key0, nk), pair * LANES:(pair + 1) * LANES]
        sp = sp_ref[slot, pair, :, r0:r1, :nk].reshape(2 * n, nk)
        r = jnp.dot(sp, tri_ref[:nk, :nk], preferred_element_type=F32)
        u = u_ref[slot, pair, :, r0:r1, :nk].reshape(2 * n, nk)
        ncarry = ncarry_ref[sub, pair, :, r0:r1, :].reshape(2 * n, LANES)
        a = jnp.exp2(u - r + jnp.concatenate([ncarry] * (nk // LANES), axis=1))
        pv = jnp.dot(a.astype(BF16), vb, preferred_element_type=F32)
        acc_ref[sub, pair, :, r0:r1, :] += pv.reshape(2, n, LANES)
        ncarry_ref[sub, pair, :, r0:r1, :] = (ncarry - r[:, :1]).reshape(2, n, LANES)


def _sb_kernel(q_ref, k_ref, v_ref, g_ref, tri_ref, yr_ref, w32_ref, x_ref, gate_ref, fg_ref, o_ref,
               qm_ref, acc_ref, ncarry_ref, u_ref, sp_ref, ys_ref, res_ref, w_ref, *, final):
    step = pl.program_id(1)
    bi = pl.program_id(0)

    @pl.when(jnp.logical_and(bi == 0, step == 0))
    def _():
        w_ref[...] = w32_ref[0].astype(BF16)

    res_ref[...] = jnp.dot(yr_ref[0], w_ref[:D_RET, :], preferred_element_type=F32)
    lane = lax.broadcasted_iota(jnp.int32, (SB_TK, LANES), 1)
    first = lane < SB_HEAD_DIM
    for sub in range(SB_SUB):
        for pair in range(SB_PAIRS):
            q = q_ref[0, sub * SB_TK:(sub + 1) * SB_TK, pair * LANES:(pair + 1) * LANES]
            zero = jnp.zeros_like(q)
            qm_ref[sub, pair, 0] = jnp.where(first, q, zero)
            qm_ref[sub, pair, 1] = jnp.where(first, zero, q)
    acc_ref[...] = jnp.zeros_like(acc_ref)
    ncarry_ref[...] = jnp.zeros_like(ncarry_ref)

    scores = functools.partial(_sb_scores, qm_ref, k_ref, u_ref, sp_ref)
    values = functools.partial(_sb_values, v_ref, tri_ref, u_ref, sp_ref, acc_ref, ncarry_ref)
    key_of = lambda blk: pl.multiple_of(blk * SB_TK, SB_TK)

    own = lambda sub: step * SB_SUB + sub
    prev_of = lambda sub: jnp.maximum(own(sub) - 1, 0) if sub == 0 else own(sub) - 1
    prev_bias = lambda sub: jnp.where(step == 0, SB_MASKED, 0.0) if sub == 0 else None
    eager, late = (0, SB_EAGER), (SB_EAGER, SB_TK)
    sequence = []
    half = SB_TK // 2
    for sub in range(SB_SUB):
        sequence.append((sub, own(sub), True, None, (0, half), half))
        sequence.append((sub, own(sub), True, None, (half, SB_TK), SB_TK))
        sequence.append((sub, prev_of(sub), False, prev_bias(sub), eager, SB_TK))
    for n, (sub, blk, diagonal, bias, rows, nk) in enumerate(sequence):
        scores(sub, n % 2, key_of(blk), diagonal, bias, rows, nk)
        if n > 0:
            p_sub, p_blk, _, _, p_rows, p_nk = sequence[n - 1]
            values(p_sub, (n - 1) % 2, key_of(p_blk), p_rows, p_nk)
    p_sub, p_blk, _, _, p_rows, p_nk = sequence[-1]
    values(p_sub, (len(sequence) - 1) % 2, key_of(p_blk), p_rows, p_nk)

    live = lambda sub, r0=0: (jnp.max(ncarry_ref[sub, :, :, r0:, :]) > -SB_DEAD).astype(jnp.int32)
    live_late = [live(sub, SB_EAGER) for sub in range(SB_SUB)]
    live_now = [live(sub) for sub in range(SB_SUB)]
    for sub in range(SB_SUB):
        @pl.when(live_late[sub] > 0)
        def _():
            scores(sub, 0, key_of(prev_of(sub)), False, prev_bias(sub), late)
            values(sub, 0, key_of(prev_of(sub)), late)

        def body(state):
            blk, _ = state
            scores(sub, 0, key_of(blk), False)
            values(sub, 0, key_of(blk))
            return blk - 1, live(sub)

        lax.while_loop(lambda state: jnp.logical_and(state[0] >= 0, state[1] > 0),
                       body, (own(sub) - 2, live_now[sub]))

    for sub in range(SB_SUB):
        rows = slice(sub * SB_TK, (sub + 1) * SB_TK)
        for pair in range(SB_PAIRS):
            lanes = slice(pair * LANES, (pair + 1) * LANES)
            y = jnp.where(first, acc_ref[sub, pair, 0], acc_ref[sub, pair, 1])
            ys_ref[rows, lanes] = (y * _silu(g_ref[0, rows, lanes])).astype(ys_ref.dtype)

    y = res_ref[...] + jnp.dot(ys_ref[...], w_ref[D_RET:, :], preferred_element_type=F32)
    xn = x_ref[0] + gate_ref[0, pl.ds(bi, 1), :] * y
    if final:
        inv = lax.rsqrt(jnp.mean(xn * xn, axis=-1, keepdims=True) + EPS)
        xn = xn * inv * fg_ref[...]
    o_ref[0] = xn.astype(o_ref.dtype)


def _stick_breaking_outproj(sb, sg, tri, y_ret, w_out, x, mod, final_g, layer):
    b, s, d = x.shape
    final = layer == DEPTH - 1
    rows = lambda width: pl.BlockSpec((1, SB_TQ, width), lambda bi, i: (bi, i, 0))
    once = lambda a: pl.BlockSpec(a.shape, lambda bi, i: (0,) * a.ndim, pipeline_mode=pl.Buffered(1))
    resident = lambda col: pl.BlockSpec((1, s, D_SB), lambda bi, i: (bi, 0, col),
                                        pipeline_mode=pl.Buffered(1))
    return pl.pallas_call(
        functools.partial(_sb_kernel, final=final),
        grid=(b, s // SB_TQ),
        in_specs=[
            rows(D_SB), resident(1), resident(2), rows(D_SB), once(tri),
            rows(D_RET),
            pl.BlockSpec((1, d, d), lambda bi, i: (layer, 0, 0), pipeline_mode=pl.Buffered(1)),
            rows(d),
            pl.BlockSpec((1, ADA_ROWS, d), lambda bi, i: (layer, 0, 2),
                         pipeline_mode=pl.Buffered(1)),
            pl.BlockSpec((1, d), lambda bi, i: (0, 0)),
        ],
        out_specs=rows(d),
        out_shape=jax.ShapeDtypeStruct((b, s, d), x.dtype),
        scratch_shapes=[
            pltpu.VMEM((SB_SUB, SB_PAIRS, 2, SB_TK, LANES), BF16),
            pltpu.VMEM((SB_SUB, SB_PAIRS, 2, SB_TK, LANES), F32),
            pltpu.VMEM((SB_SUB, SB_PAIRS, 2, SB_TK, LANES), F32),
            pltpu.VMEM((2, SB_PAIRS, 2, SB_TK, SB_TK), F32),
            pltpu.VMEM((2, SB_PAIRS, 2, SB_TK, SB_TK), BF16),
            pltpu.VMEM((SB_TQ, D_SB), BF16),
            pltpu.VMEM((SB_TQ, D_MODEL), F32),
            pltpu.VMEM((D_MODEL, D_MODEL), BF16),
        ],
        compiler_params=pltpu.CompilerParams(dimension_semantics=("arbitrary", "arbitrary"),
                                             vmem_limit_bytes=56 * MIB),
        name="stick_breaking_out_proj_final" if final else "stick_breaking_out_proj",
    )(sb, sb, sb, sg, tri, y_ret, w_out, x, mod, final_g.reshape(1, d))


def kernel(x, c, norm_g, w_ada, b_ada, w_in, w_out, final_g):
    b, s, d = x.shape
    assert d == D_MODEL and s % max(SB_TQ, PROJ_TM) == 0
    c_pad = jnp.zeros((ADA_ROWS, d), F32).at[:b].set(c.astype(F32))
    mod = _ada(c_pad, w_ada, b_ada)
    tables = _retention_tables(s)
    tri = jnp.asarray(np.tril(np.ones((SB_TK, SB_TK))), dtype=BF16)

    h = x.astype(F32)
    for layer in range(DEPTH):
        y_ret, sb, sg = _inproj(h, norm_g, mod, w_in, tables, layer)
        h = _stick_breaking_outproj(sb, sg, tri, y_ret, w_out, h, mod, final_g, layer)
    return h.astype(x.dtype)
```

```python
import functools

import jax
import jax.numpy as jnp
import numpy as np
from jax import lax
from jax.experimental import pallas as pl
from jax.experimental.pallas import tpu as pltpu

D_MODEL = 1024
DEPTH = 2
CHUNK = 64
D_RET = 512
D_SB = 512
RET_HEADS = 4
RET_HEAD_DIM = 128
SB_HEADS = 8
SB_HEAD_DIM = 64
ROPE_BASE = 10000.0
EPS = 1e-6
LOG2E = 1.4426950408889634
LOG2E_HI = 1.4453125
LOG2E_LO = LOG2E - LOG2E_HI

LANES = 128
SB_PAIRS = SB_HEADS * SB_HEAD_DIM // LANES
MIB = 1024 * 1024

ADA_ROWS = 8
ADA_TN = 1536
PROJ_TN = 512
RET_L = 256
PROJ_TM = 2 * RET_L
SB_TK = 256
SB_SUB = 2
SB_TQ = SB_SUB * SB_TK
SB_EAGER = 160

SB_EXP2_CLAMP = 126.0
SB_DEAD = 151.0
SB_MASKED = -1e30
SB_Q_SCALE = LOG2E * SB_HEAD_DIM ** -0.5

F32 = jnp.float32
BF16 = jnp.bfloat16


def _silu(x):
    return x * (1.0 / (1.0 + jnp.exp(-x)))


def _ada_kernel(c_ref, w_ref, b_ref, o_ref):
    c_act = _silu(c_ref[...])
    o_ref[0] = jnp.dot(c_act.astype(BF16), w_ref[0].astype(BF16),
                       preferred_element_type=F32) + b_ref[0]


def _ada(c_pad, w_ada, b_ada):
    n = 3 * D_MODEL
    return pl.pallas_call(
        _ada_kernel,
        grid=(DEPTH, n // ADA_TN),
        in_specs=[
            pl.BlockSpec((ADA_ROWS, D_MODEL), lambda l, j: (0, 0)),
            pl.BlockSpec((1, D_MODEL, ADA_TN), lambda l, j: (l, 0, j)),
            pl.BlockSpec((1, 1, ADA_TN), lambda l, j: (l, 0, j)),
        ],
        out_specs=pl.BlockSpec((1, ADA_ROWS, ADA_TN), lambda l, j: (l, 0, j)),
        out_shape=jax.ShapeDtypeStruct((DEPTH, ADA_ROWS, n), F32),
        name="ada_mod",
    )(c_pad, w_ada, b_ada.reshape(DEPTH, 1, n))


def _retention_tables(seq):
    half = RET_HEAD_DIM // 2
    pos = np.arange(seq, dtype=np.float64)
    inv = ROPE_BASE ** (-np.arange(half, dtype=np.float64) / half)
    ang = pos[:, None] * inv[None, :]
    cos, sin = np.cos(ang), np.sin(ang)
    cos2 = np.concatenate([cos, cos], axis=-1)
    sin2 = np.concatenate([-sin, sin], axis=-1)

    log_gamma = np.log1p(-(2.0 ** (-5.0 - np.arange(RET_HEADS, dtype=np.float64))))
    idx = np.arange(RET_L, dtype=np.float64)
    diff = idx[:, None] - idx[None, :]
    same = (np.arange(RET_L)[:, None] // CHUNK) == (np.arange(RET_L)[None, :] // CHUNK)
    expo = np.where(same, np.abs(diff), diff)
    dmask = np.where(same | (diff > 0), np.exp(expo[None] * log_gamma[:, None, None]), 0.0)
    q_dec = np.exp((idx + 1.0)[None, :] * log_gamma[:, None])
    k_dec = np.exp((RET_L - 1.0 - idx)[None, :] * log_gamma[:, None])
    blk_dec = np.exp(RET_L * log_gamma)
    rep = lambda t: np.broadcast_to(t[:, :, None], (RET_HEADS, RET_L, LANES))
    blk = np.broadcast_to(blk_dec[:, None, None], (RET_HEADS, 1, LANES))
    return tuple(jnp.asarray(t, dtype=F32) for t in (cos2, sin2, dmask, rep(q_dec), rep(k_dec), blk))


def _retention_head(h, rq, rk, rv, rg, cos, sin, dm_ref, qd_ref, kd_ref, bd_ref, state_ref):
    half = RET_HEAD_DIM // 2
    cols = slice(h * RET_HEAD_DIM, (h + 1) * RET_HEAD_DIM)
    q = rq[:, cols]
    k = rk[:, cols]
    q = q * cos + pltpu.roll(q, half, axis=1) * sin
    k = (k * cos + pltpu.roll(k, half, axis=1) * sin) * (RET_HEAD_DIM ** -0.5)
    vb = rv[:, cols].astype(BF16)
    qb = q.astype(BF16)
    scores = lax.dot_general(qb, k.astype(BF16), (((1,), (1,)), ((), ())),
                             preferred_element_type=F32) * dm_ref[h]
    o = jnp.dot(scores.astype(BF16), vb, preferred_element_type=F32)
    state = state_ref[h]
    o = o + jnp.dot(qb, state.astype(BF16), preferred_element_type=F32) * qd_ref[h]
    kv = lax.dot_general((k * kd_ref[h]).astype(BF16), vb, (((0,), (0,)), ((), ())),
                         preferred_element_type=F32)
    state_ref[h] = state * bd_ref[h] + kv
    mu = jnp.mean(o, axis=-1, keepdims=True)
    oc = o - mu
    var = jnp.mean(oc * oc, axis=-1, keepdims=True)
    return oc * lax.rsqrt(var + EPS) * _silu(rg[:, cols])


def _inproj_kernel(x_ref, g_ref, shift_ref, scale_ref, w32_ref, cos_ref, sin_ref, dm_ref, qd_ref,
                   kd_ref, bd_ref, yret_ref, sb_ref, sg_ref, state_ref, w_ref):
    bi = pl.program_id(0)

    @pl.when(jnp.logical_and(bi == 0, pl.program_id(1) == 0))
    def _():
        w_ref[...] = w32_ref[0].astype(BF16)

    @pl.when(pl.program_id(1) == 0)
    def _():
        state_ref[...] = jnp.zeros_like(state_ref)

    x = x_ref[0]
    inv = lax.rsqrt(jnp.mean(x * x, axis=-1, keepdims=True) + EPS)
    scale = scale_ref[0, pl.ds(bi, 1), :]
    shift = shift_ref[0, pl.ds(bi, 1), :]
    h = (x * inv * g_ref[0]) * (1.0 + scale) + shift
    hb = h.astype(BF16)
    proj = lambda n: jnp.dot(hb, w_ref[:, n * PROJ_TN:(n + 1) * PROJ_TN], preferred_element_type=F32)
    n_ret = 4 * D_RET // PROJ_TN
    n_qkv = 3 * D_SB // PROJ_TN
    ret = [proj(n) for n in range(n_ret)]
    for blk in range(PROJ_TM // RET_L):
        rows = slice(blk * RET_L, (blk + 1) * RET_L)
        rq, rk, rv, rg = [r[rows] for r in ret]
        cos = cos_ref[rows, :]
        sin = sin_ref[rows, :]
        for hd in range(RET_HEADS):
            y = _retention_head(hd, rq, rk, rv, rg, cos, sin, dm_ref, qd_ref, kd_ref, bd_ref,
                                state_ref)
            yret_ref[0, rows, hd * RET_HEAD_DIM:(hd + 1) * RET_HEAD_DIM] = y.astype(yret_ref.dtype)
    for n in range(n_qkv):
        p = proj(n_ret + n)
        if n * PROJ_TN < D_SB:
            p = p * SB_Q_SCALE
        sb_ref[0, :, n * PROJ_TN:(n + 1) * PROJ_TN] = p.astype(BF16)
    sg_ref[0] = proj(n_ret + n_qkv)


def _inproj(x, norm_g, mod, w_in, tables, layer):
    b, s, d = x.shape
    d_in = w_in.shape[2]
    cos2, sin2, dmask, q_dec, k_dec, blk = tables
    once = lambda a: pl.BlockSpec(a.shape, lambda bi, i: (0,) * a.ndim, pipeline_mode=pl.Buffered(1))
    of_layer = lambda shape, col=0: pl.BlockSpec((1,) + shape, lambda bi, i: (layer, 0, col),
                                                 pipeline_mode=pl.Buffered(1))
    return pl.pallas_call(
        _inproj_kernel,
        grid=(b, s // PROJ_TM),
        in_specs=[
            pl.BlockSpec((1, PROJ_TM, d), lambda bi, i: (bi, i, 0)),
            of_layer((1, d)),
            of_layer((ADA_ROWS, d), 0), of_layer((ADA_ROWS, d), 1),
            of_layer((d, d_in)),
            pl.BlockSpec((PROJ_TM, LANES), lambda bi, i: (i, 0)),
            pl.BlockSpec((PROJ_TM, LANES), lambda bi, i: (i, 0)),
            once(dmask), once(q_dec), once(k_dec), once(blk),
        ],
        out_specs=[
            pl.BlockSpec((1, PROJ_TM, D_RET), lambda bi, i: (bi, i, 0)),
            pl.BlockSpec((1, PROJ_TM, 3 * D_SB), lambda bi, i: (bi, i, 0)),
            pl.BlockSpec((1, PROJ_TM, D_SB), lambda bi, i: (bi, i, 0)),
        ],
        out_shape=[
            jax.ShapeDtypeStruct((b, s, D_RET), BF16),
            jax.ShapeDtypeStruct((b, s, 3 * D_SB), BF16),
            jax.ShapeDtypeStruct((b, s, D_SB), F32),
        ],
        scratch_shapes=[pltpu.VMEM((RET_HEADS, RET_HEAD_DIM, RET_HEAD_DIM), F32),
                        pltpu.VMEM((d, d_in), BF16)],
        compiler_params=pltpu.CompilerParams(dimension_semantics=("arbitrary", "arbitrary"),
                                             vmem_limit_bytes=56 * MIB),
        name="in_proj_retention",
    )(x, norm_g.reshape(DEPTH, 1, d), mod, mod, w_in, cos2, sin2, dmask, q_dec, k_dec, blk)


def _softplus2(u):
    ln = jnp.log(1.0 + jnp.exp2(jnp.minimum(u, SB_EXP2_CLAMP)))
    return jnp.maximum(u, ln * LOG2E_HI + ln * LOG2E_LO)


def _sb_scores(qm_ref, k_ref, u_ref, sp_ref, sub, slot, key0, diagonal, bias=None,
               rows=(0, SB_TK), nk=SB_TK):
    r0, r1 = rows
    n = r1 - r0
    assert not diagonal or nk >= r1
    for pair in range(SB_PAIRS):
        kb = k_ref[0, pl.ds(key0, nk), pair * LANES:(pair + 1) * LANES]
        qm = qm_ref[sub, pair, :, r0:r1, :].reshape(2 * n, LANES)
        u = lax.dot_general(qm, kb, (((1,), (1,)), ((), ())), preferred_element_type=F32)
        if bias is not None:
            u = u + bias
        sp = _softplus2(u.astype(BF16))
        if diagonal:
            t_loc = lax.broadcasted_iota(jnp.int32, (n, nk), 0) + r0
            s_loc = lax.broadcasted_iota(jnp.int32, (n, nk), 1)
            m = s_loc < t_loc
            mask = jnp.concatenate([m, m], axis=0)
            sp = jnp.where(mask, sp, jnp.zeros_like(sp))
            u = jnp.where(mask, u, SB_MASKED)
        u_ref[slot, pair, :, r0:r1, :nk] = u.reshape(2, n, nk)
        sp_ref[slot, pair, :, r0:r1, :nk] = sp.reshape(2, n, nk)


def _sb_values(v_ref, tri_ref, u_ref, sp_ref, acc_ref, ncarry_ref, sub, slot, key0,
               rows=(0, SB_TK), nk=SB_TK):
    r0, r1 = rows
    n = r1 - r0
    for pair in range(SB_PAIRS):
        vb = v_ref[0, pl.ds(key0, nk), pair * LANES:(pair + 1) * LANES]
        sp = sp_ref[slot, pair, :, r0:r1, :nk].reshape(2 * n, nk)
        r = jnp.dot(sp, tri_ref[:nk, :nk], preferred_element_type=F32)
        u = u_ref[slot, pair, :, r0:r1, :nk].reshape(2 * n, nk)
        ncarry = ncarry_ref[sub, pair, :, r0:r1, :].reshape(2 * n, LANES)
        a = jnp.exp2(u - r + jnp.concatenate([ncarry] * (nk // LANES), axis=1))
        pv = jnp.dot(a.astype(BF16), vb, preferred_element_type=F32)
        acc_ref[sub, pair, :, r0:r1, :] += pv.reshape(2, n, LANES)
        ncarry_ref[sub, pair, :, r0:r1, :] = (ncarry - r[:, :1]).reshape(2, n, LANES)


def _sb_kernel(q_ref, k_ref, v_ref, g_ref, tri_ref, yr_ref, w32_ref, x_ref, gate_ref, fg_ref, o_ref,
               qm_ref, acc_ref, ncarry_ref, u_ref, sp_ref, ys_ref, res_ref, w_ref, *, final):
    step = pl.program_id(1)
    bi = pl.program_id(0)

    @pl.when(jnp.logical_and(bi == 0, step == 0))
    def _():
        w_ref[...] = w32_ref[0].astype(BF16)

    res_ref[...] = jnp.dot(yr_ref[0], w_ref[:D_RET, :], preferred_element_type=F32)
    lane = lax.broadcasted_iota(jnp.int32, (SB_TK, LANES), 1)
    first = lane < SB_HEAD_DIM
    for sub in range(SB_SUB):
        for pair in range(SB_PAIRS):
            q = q_ref[0, sub * SB_TK:(sub + 1) * SB_TK, pair * LANES:(pair + 1) * LANES]
            zero = jnp.zeros_like(q)
            qm_ref[sub, pair, 0] = jnp.where(first, q, zero)
            qm_ref[sub, pair, 1] = jnp.where(first, zero, q)
    acc_ref[...] = jnp.zeros_like(acc_ref)
    ncarry_ref[...] = jnp.zeros_like(ncarry_ref)

    scores = functools.partial(_sb_scores, qm_ref, k_ref, u_ref, sp_ref)
    values = functools.partial(_sb_values, v_ref, tri_ref, u_ref, sp_ref, acc_ref, ncarry_ref)
    key_of = lambda blk: pl.multiple_of(blk * SB_TK, SB_TK)

    own = lambda sub: step * SB_SUB + sub
    prev_of = lambda sub: jnp.maximum(own(sub) - 1, 0) if sub == 0 else own(sub) - 1
    prev_bias = lambda sub: jnp.where(step == 0, SB_MASKED, 0.0) if sub == 0 else None
    eager, late = (0, SB_EAGER), (SB_EAGER, SB_TK)
    sequence = []
    half = SB_TK // 2
    for sub in range(SB_SUB):
        sequence.append((sub, own(sub), True, None, (0, half), half))
        sequence.append((sub, own(sub), True, None, (half, SB_TK), SB_TK))
        sequence.append((sub, prev_of(sub), False, prev_bias(sub), eager, SB_TK))
    for n, (sub, blk, diagonal, bias, rows, nk) in enumerate(sequence):
        scores(sub, n % 2, key_of(blk), diagonal, bias, rows, nk)
        if n > 0:
            p_sub, p_blk, _, _, p_rows, p_nk = sequence[n - 1]
            values(p_sub, (n - 1) % 2, key_of(p_blk), p_rows, p_nk)
    p_sub, p_blk, _, _, p_rows, p_nk = sequence[-1]
    values(p_sub, (len(sequence) - 1) % 2, key_of(p_blk), p_rows, p_nk)

    live =lambda sub, r0=0: (jnp.max(ncarry_ref[sub, :, :, r0:, :]) > -SB_DEAD).astype(jnp.int32)
    live_late = [live(sub, SB_EAGER) for sub in range(SB_SUB)]
    live_now = [live(sub) for sub in range(SB_SUB)]
    for sub in range(SB_SUB):
        @pl.when(live_late[sub] > 0)
        def _():
            scores(sub, 0, key_of(prev_of(sub)), False, prev_bias(sub), late)
            values(sub, 0, key_of(prev_of(sub)), late)

        def body(state):
            blk, _ = state
            scores(sub, 0, key_of(blk), False)
            values(sub, 0, key_of(blk))
            return blk - 1, live(sub)

        lax.while_loop(lambda state: jnp.logical_and(state[0] >= 0, state[1] > 0),
                       body, (own(sub) - 2, live_now[sub]))

    for sub in range(SB_SUB):
        rows = slice(sub * SB_TK, (sub + 1) * SB_TK)
        for pair in range(SB_PAIRS):
            lanes = slice(pair * LANES, (pair + 1) * LANES)
            y = jnp.where(first, acc_ref[sub, pair, 0], acc_ref[sub, pair, 1])
            ys_ref[rows, lanes] = (y * _silu(g_ref[0, rows, lanes])).astype(ys_ref.dtype)

    y = res_ref[...] + jnp.dot(ys_ref[...], w_ref[D_RET:, :], preferred_element_type=F32)
    xn = x_ref[0] + gate_ref[0, pl.ds(bi, 1), :] * y
    if final:
        inv = lax.rsqrt(jnp.mean(xn * xn, axis=-1, keepdims=True) + EPS)
        xn = xn * inv * fg_ref[...]
    o_ref[0] = xn.astype(o_ref.dtype)


def _stick_breaking_outproj(sb, sg, tri, y_ret, w_out, x, mod, final_g, layer):
    b, s, d = x.shape
    final = layer == DEPTH - 1
    rows = lambda width: pl.BlockSpec((1, SB_TQ, width), lambda bi, i: (bi, i, 0))
    once = lambda a: pl.BlockSpec(a.shape, lambda bi, i: (0,) * a.ndim, pipeline_mode=pl.Buffered(1))
    resident = lambda col: pl.BlockSpec((1, s, D_SB), lambda bi, i: (bi, 0, col),
                                        pipeline_mode=pl.Buffered(1))
    return pl.pallas_call(
        functools.partial(_sb_kernel, final=final),
        grid=(b, s // SB_TQ),
        in_specs=[
            rows(D_SB), resident(1), resident(2), rows(D_SB), once(tri),
            rows(D_RET),
            pl.BlockSpec((1, d, d), lambda bi, i: (layer, 0, 0), pipeline_mode=pl.Buffered(1)),
            rows(d),
            pl.BlockSpec((1, ADA_ROWS, d), lambda bi, i: (layer, 0, 2),
                         pipeline_mode=pl.Buffered(1)),
            pl.BlockSpec((1, d), lambda bi, i: (0, 0)),
        ],
        out_specs=rows(d),
        out_shape=jax.ShapeDtypeStruct((b, s, d), x.dtype),
        scratch_shapes=[
            pltpu.VMEM((SB_SUB, SB_PAIRS, 2, SB_TK, LANES), BF16),
            pltpu.VMEM((SB_SUB, SB_PAIRS, 2, SB_TK, LANES), F32),
            pltpu.VMEM((SB_SUB, SB_PAIRS, 2, SB_TK, LANES), F32),
            pltpu.VMEM((2, SB_PAIRS, 2, SB_TK, SB_TK), F32),
            pltpu.VMEM((2, SB_PAIRS, 2, SB_TK, SB_TK), BF16),
            pltpu.VMEM((SB_TQ, D_SB), BF16),
            pltpu.VMEM((SB_TQ, D_MODEL), F32),
            pltpu.VMEM((D_MODEL, D_MODEL), BF16),
        ],
        compiler_params=pltpu.CompilerParams(dimension_semantics=("arbitrary", "arbitrary"),
                                             vmem_limit_bytes=56 * MIB),
        name="stick_breaking_out_proj_final" if final else "stick_breaking_out_proj",
    )(sb, sb, sb, sg, tri, y_ret, w_out, x, mod, final_g.reshape(1, d))


def kernel(x, c, norm_g, w_ada, b_ada, w_in, w_out, final_g):
    b, s, d = x.shape
    assert d == D_MODEL and s % max(SB_TQ, PROJ_TM) == 0
    c_pad = jnp.zeros((ADA_ROWS, d), F32).at[:b].set(c.astype(F32))
    mod = _ada(c_pad, w_ada, b_ada)
    tables = _retention_tables(s)
    tri = jnp.asarray(np.tril(np.ones((SB_TK, SB_TK))), dtype=BF16)

    h = x.astype(F32)
    for layer in range(DEPTH):
        y_ret, sb, sg = _inproj(h, norm_g, mod, w_in, tables, layer)
        h = _stick_breaking_outproj(sb, sg, tri, y_ret, w_out, h, mod, final_g, layer)
    return h.astype(x.dtype)
```

```python
import functools

import jax
import jax.numpy as jnp
import numpy as np
from jax import lax
from jax.experimental import pallas as pl
from jax.experimental.pallas import tpu as pltpu

D_MODEL = 1024
DEPTH = 2
CHUNK = 64
D_RET = 512
D_SB = 512
RET_HEADS = 4
RET_HEAD_DIM = 128
SB_HEADS = 8
SB_HEAD_DIM = 64
ROPE_BASE = 10000.0
EPS = 1e-6
LOG2E = 1.4426950408889634
LOG2E_HI = 1.4453125
LOG2E_LO = LOG2E - LOG2E_HI

LANES = 128
SB_PAIRS = SB_HEADS * SB_HEAD_DIM // LANES
MIB = 1024 * 1024

ADA_ROWS = 8
ADA_TN = 1536
PROJ_TN = 512
RET_L = 256
PROJ_TM = 2 * RET_L
SB_TK = 256
SB_SUB = 2
SB_TQ = SB_SUB * SB_TK
SB_EAGER = 192

SB_EXP2_CLAMP = 126.0
SB_DEAD = 151.0
SB_MASKED = -1e30
SB_Q_SCALE = LOG2E * SB_HEAD_DIM ** -0.5

F32 = jnp.float32
BF16 = jnp.bfloat16


def _silu(x):
    return x * (1.0 / (1.0 + jnp.exp(-x)))


def _ada_kernel(c_ref, w_ref, b_ref, o_ref):
    c_act = _silu(c_ref[...])
    o_ref[0] = jnp.dot(c_act.astype(BF16), w_ref[0].astype(BF16),
                       preferred_element_type=F32) + b_ref[0]


def _ada(c_pad, w_ada, b_ada):
    n = 3 * D_MODEL
    return pl.pallas_call(
        _ada_kernel,
        grid=(DEPTH, n // ADA_TN),
        in_specs=[
            pl.BlockSpec((ADA_ROWS, D_MODEL), lambda l, j: (0, 0)),
            pl.BlockSpec((1, D_MODEL, ADA_TN), lambda l, j: (l, 0, j)),
            pl.BlockSpec((1, 1, ADA_TN), lambda l, j: (l, 0, j)),
        ],
        out_specs=pl.BlockSpec((1, ADA_ROWS, ADA_TN), lambda l, j: (l, 0, j)),
        out_shape=jax.ShapeDtypeStruct((DEPTH, ADA_ROWS, n), F32),
        name="ada_mod",
    )(c_pad, w_ada, b_ada.reshape(DEPTH, 1, n))


def _retention_tables(seq):
    half = RET_HEAD_DIM // 2
    pos = np.arange(seq, dtype=np.float64)
    inv = ROPE_BASE ** (-np.arange(half, dtype=np.float64) / half)
    ang = pos[:, None] * inv[None, :]
    cos, sin = np.cos(ang), np.sin(ang)
    cos2 = np.concatenate([cos, cos], axis=-1)
    sin2 = np.concatenate([-sin, sin], axis=-1)

    log_gamma = np.log1p(-(2.0 ** (-5.0 - np.arange(RET_HEADS, dtype=np.float64))))
    idx = np.arange(RET_L, dtype=np.float64)
    diff = idx[:, None] - idx[None, :]
    same = (np.arange(RET_L)[:, None] // CHUNK) == (np.arange(RET_L)[None, :] // CHUNK)
    expo = np.where(same, np.abs(diff), diff)
    dmask = np.where(same | (diff > 0), np.exp(expo[None] * log_gamma[:, None, None]), 0.0)
    q_dec = np.exp((idx + 1.0)[None, :] * log_gamma[:, None])
    k_dec = np.exp((RET_L - 1.0 - idx)[None, :] * log_gamma[:, None])
    blk_dec = np.exp(RET_L * log_gamma)
    rep = lambda t: np.broadcast_to(t[:, :, None], (RET_HEADS, RET_L, LANES))
    blk = np.broadcast_to(blk_dec[:, None, None], (RET_HEADS, 1, LANES))
    return tuple(jnp.asarray(t, dtype=F32) for t in (cos2, sin2, dmask, rep(q_dec), rep(k_dec), blk))


def _retention_head(h, rq, rk, rv, rg, cos, sin, dm_ref, qd_ref, kd_ref, bd_ref, state_ref):
    half = RET_HEAD_DIM // 2
    cols = slice(h * RET_HEAD_DIM, (h + 1) * RET_HEAD_DIM)
    q = rq[:, cols]
    k = rk[:, cols]
    q = q * cos + pltpu.roll(q, half, axis=1) * sin
    k = (k * cos + pltpu.roll(k, half, axis=1) * sin) * (RET_HEAD_DIM ** -0.5)
    vb = rv[:, cols].astype(BF16)
    qb = q.astype(BF16)
    scores = lax.dot_general(qb, k.astype(BF16), (((1,), (1,)), ((), ())),
                             preferred_element_type=F32) * dm_ref[h]
    o = jnp.dot(scores.astype(BF16), vb, preferred_element_type=F32)
    state = state_ref[h]
    o = o + jnp.dot(qb, state.astype(BF16), preferred_element_type=F32) * qd_ref[h]
    kv = lax.dot_general((k * kd_ref[h]).astype(BF16), vb, (((0,), (0,)), ((), ())),
                         preferred_element_type=F32)
    state_ref[h] = state * bd_ref[h] + kv
    mu = jnp.mean(o, axis=-1, keepdims=True)
    oc = o - mu
    var = jnp.mean(oc * oc, axis=-1, keepdims=True)
    return oc * lax.rsqrt(var + EPS) * _silu(rg[:, cols])


def _inproj_kernel(x_ref, g_ref, shift_ref, scale_ref, w32_ref, cos_ref, sin_ref, dm_ref, qd_ref,
                   kd_ref, bd_ref, yret_ref, sb_ref, sg_ref, state_ref, w_ref):
    bi = pl.program_id(0)

    @pl.when(jnp.logical_and(bi == 0, pl.program_id(1) == 0))
    def _():
        w_ref[...] = w32_ref[0].astype(BF16)

    @pl.when(pl.program_id(1) == 0)
    def _():
        state_ref[...] = jnp.zeros_like(state_ref)

    x = x_ref[0]
    inv = lax.rsqrt(jnp.mean(x * x, axis=-1, keepdims=True) + EPS)
    scale = scale_ref[0, pl.ds(bi, 1), :]
    shift = shift_ref[0, pl.ds(bi, 1), :]
    h = (x * inv * g_ref[0]) * (1.0 + scale) + shift
    hb = h.astype(BF16)
    proj = lambda n: jnp.dot(hb, w_ref[:, n * PROJ_TN:(n + 1) * PROJ_TN], preferred_element_type=F32)
    n_ret = 4 * D_RET // PROJ_TN
    n_qkv = 3 * D_SB // PROJ_TN
    ret = [proj(n) for n in range(n_ret)]
    for blk in range(PROJ_TM // RET_L):
        rows = slice(blk * RET_L, (blk + 1) * RET_L)
        rq, rk, rv, rg = [r[rows] for r in ret]
        cos = cos_ref[rows, :]
        sin = sin_ref[rows, :]
        for hd in range(RET_HEADS):
            y = _retention_head(hd, rq, rk, rv, rg, cos, sin, dm_ref, qd_ref, kd_ref, bd_ref,
                                state_ref)
            yret_ref[0, rows, hd * RET_HEAD_DIM:(hd + 1) * RET_HEAD_DIM] = y.astype(yret_ref.dtype)
    for n in range(n_qkv):
        p = proj(n_ret + n)
        if n * PROJ_TN < D_SB:
            p = p * SB_Q_SCALE
        sb_ref[0, :, n * PROJ_TN:(n + 1) * PROJ_TN] = p.astype(BF16)
    sg_ref[0] = proj(n_ret + n_qkv)


def _inproj(x, norm_g, mod, w_in, tables, layer):
    b, s, d = x.shape
    d_in = w_in.shape[2]
    cos2, sin2, dmask, q_dec, k_dec, blk = tables
    once = lambda a: pl.BlockSpec(a.shape, lambda bi, i: (0,) * a.ndim, pipeline_mode=pl.Buffered(1))
    of_layer = lambda shape, col=0: pl.BlockSpec((1,) + shape, lambda bi, i: (layer, 0, col),
                                                 pipeline_mode=pl.Buffered(1))
    return pl.pallas_call(
        _inproj_kernel,
        grid=(b, s // PROJ_TM),
        in_specs=[
            pl.BlockSpec((1, PROJ_TM, d), lambda bi, i: (bi, i, 0)),
            of_layer((1, d)),
            of_layer((ADA_ROWS, d), 0), of_layer((ADA_ROWS, d), 1),
            of_layer((d, d_in)),
            pl.BlockSpec((PROJ_TM, LANES), lambda bi, i: (i, 0)),
            pl.BlockSpec((PROJ_TM, LANES), lambda bi, i: (i, 0)),
            once(dmask), once(q_dec), once(k_dec), once(blk),
        ],
        out_specs=[
            pl.BlockSpec((1, PROJ_TM, D_RET), lambda bi, i: (bi, i, 0)),
            pl.BlockSpec((1, PROJ_TM, 3 * D_SB), lambda bi, i: (bi, i, 0)),
            pl.BlockSpec((1, PROJ_TM, D_SB), lambda bi, i: (bi, i, 0)),
        ],
        out_shape=[
            jax.ShapeDtypeStruct((b, s, D_RET), BF16),
            jax.ShapeDtypeStruct((b, s, 3 * D_SB), BF16),
            jax.ShapeDtypeStruct((b, s, D_SB), F32),
        ],
        scratch_shapes=[pltpu.VMEM((RET_HEADS, RET_HEAD_DIM, RET_HEAD_DIM), F32),
                        pltpu.VMEM((d, d_in), BF16)],
        compiler_params=pltpu.CompilerParams(dimension_semantics=("arbitrary", "arbitrary"),
                                             vmem_limit_bytes=56 * MIB),
        name="in_proj_retention",
    )(x, norm_g.reshape(DEPTH, 1, d), mod, mod, w_in, cos2, sin2, dmask, q_dec, k_dec, blk)


def _softplus2(u):
    ln = jnp.log(1.0 + jnp.exp2(jnp.minimum(u, SB_EXP2_CLAMP)))
    return jnp.maximum(u, ln * LOG2E_HI + ln * LOG2E_LO)


def _sb_scores(qm_ref, k_ref, u_ref, sp_ref, sub, slot, key0, diagonal, bias=None,
               rows=(0, SB_TK), nk=SB_TK):
    r0, r1 = rows
    n = r1 - r0
    assert not diagonal or nk >= r1
    for pair in range(SB_PAIRS):
        kb = k_ref[0, pl.ds(key0, nk), pair * LANES:(pair + 1) * LANES]
        qm = qm_ref[sub, pair, :, r0:r1, :].reshape(2 * n, LANES)
        u = lax.dot_general(qm, kb, (((1,), (1,)), ((), ())), preferred_element_type=F32)
        if bias is not None:
            u = u + bias
        sp = _softplus2(u.astype(BF16))
        if diagonal:
            t_loc = lax.broadcasted_iota(jnp.int32, (n, nk), 0) + r0
            s_loc = lax.broadcasted_iota(jnp.int32, (n, nk), 1)
            m = s_loc < t_loc
            mask = jnp.concatenate([m, m], axis=0)
            sp = jnp.where(mask, sp, jnp.zeros_like(sp))
            u = jnp.where(mask, u, SB_MASKED)
        u_ref[slot, pair, :, r0:r1, :nk] = u.reshape(2, n, nk)
        sp_ref[slot, pair, :, r0:r1, :nk] = sp.reshape(2, n, nk)


def _sb_values(v_ref, tri_ref, u_ref, sp_ref, acc_ref, ncarry_ref, sub, slot, key0,
               rows=(0, SB_TK), nk=SB_TK):
    r0, r1 = rows
    n = r1 - r0
    for pair in range(SB_PAIRS):
        vb = v_ref[0, pl.ds(key0, nk), pair * LANES:(pair + 1) * LANES]
        sp = sp_ref[slot, pair, :, r0:r1, :nk].reshape(2 * n, nk)
        r = jnp.dot(sp, tri_ref[:nk, :nk], preferred_element_type=F32)
        u = u_ref[slot, pair, :, r0:r1, :nk].reshape(2 * n, nk)
        ncarry = ncarry_ref[sub, pair, :, r0:r1, :].reshape(2 * n, LANES)
        a = jnp.exp2(u - r + jnp.concatenate([ncarry] * (nk // LANES), axis=1))
        pv = jnp.dot(a.astype(BF16), vb, preferred_element_type=F32)
        acc_ref[sub, pair, :, r0:r1, :] += pv.reshape(2, n, LANES)
        ncarry_ref[sub, pair, :, r0:r1, :] = (ncarry - r[:, :1]).reshape(2, n, LANES)


def _sb_kernel(q_ref, k_ref, v_ref, g_ref, tri_ref, yr_ref, w32_ref, x_ref, gate_ref, fg_ref, o_ref,
               qm_ref, acc_ref, ncarry_ref, u_ref, sp_ref, ys_ref, res_ref, w_ref, *, final):
    step = pl.program_id(1)
    bi = pl.program_id(0)

    @pl.when(jnp.logical_and(bi == 0, step == 0))
    def _():
        w_ref[...] = w32_ref[0].astype(BF16)

    res_ref[...] = jnp.dot(yr_ref[0], w_ref[:D_RET, :], preferred_element_type=F32)
    lane = lax.broadcasted_iota(jnp.int32, (SB_TK, LANES), 1)
    first = lane < SB_HEAD_DIM
    for sub in range(SB_SUB):
        for pair in range(SB_PAIRS):
            q = q_ref[0, sub * SB_TK:(sub + 1) * SB_TK, pair * LANES:(pair + 1) * LANES]
            zero = jnp.zeros_like(q)
            qm_ref[sub, pair, 0] = jnp.where(first, q, zero)
            qm_ref[sub, pair, 1] = jnp.where(first, zero, q)
    acc_ref[...] = jnp.zeros_like(acc_ref)
    ncarry_ref[...] = jnp.zeros_like(ncarry_ref)

    scores = functools.partial(_sb_scores, qm_ref, k_ref, u_ref, sp_ref)
    values = functools.partial(_sb_values, v_ref, tri_ref, u_ref, sp_ref, acc_ref, ncarry_ref)
    key_of = lambda blk: pl.multiple_of(blk * SB_TK, SB_TK)

    own = lambda sub: step * SB_SUB + sub
    prev_of = lambda sub: jnp.maximum(own(sub) - 1, 0) if sub == 0 else own(sub) - 1
    prev_bias = lambda sub: jnp.where(step == 0, SB_MASKED, 0.0) if sub == 0 else None
    eager, late = (0, SB_EAGER), (SB_EAGER, SB_TK)
    sequence = []
    half = SB_TK // 2
    for sub in range(SB_SUB):
        sequence.append((sub, own(sub), True, None, (0, half), half))
        sequence.append((sub, own(sub), True, None, (half, SB_TK), SB_TK))
        sequence.append((sub, prev_of(sub), False, prev_bias(sub), eager, SB_TK))
    for n, (sub, blk, diagonal, bias, rows, nk) in enumerate(sequence):
        scores(sub, n % 2, key_of(blk), diagonal, bias, rows, nk)
        if n > 0:
            p_sub, p_blk, _, _, p_rows, p_nk = sequence[n - 1]
            values(p_sub, (n - 1) % 2, key_of(p_blk), p_rows, p_nk)
    p_sub, p_blk, _, _, p_rows, p_nk = sequence[-1]
    values(p_sub, (len(sequence) - 1) % 2, key_of(p_blk), p_rows, p_nk)

    live =lambda sub, r0=0: (jnp.max(ncarry_ref[sub, :, :, r0:, :]) > -SB_DEAD).astype(jnp.int32)
    live_late = [live(sub, SB_EAGER) for sub in range(SB_SUB)]
    live_now = [live(sub) for sub in range(SB_SUB)]
    for sub in range(SB_SUB):
        @pl.when(live_late[sub] > 0)
        def _():
            scores(sub, 0, key_of(prev_of(sub)), False, prev_bias(sub), late)
            values(sub, 0, key_of(prev_of(sub)), late)

        def body(state):
            blk, _ = state
            scores(sub, 0, key_of(blk), False)
            values(sub, 0, key_of(blk))
            return blk - 1, live(sub)

        lax.while_loop(lambda state: jnp.logical_and(state[0] >= 0, state[1] > 0),
                       body, (own(sub) - 2, live_now[sub]))

    for sub in range(SB_SUB):
        rows = slice(sub * SB_TK, (sub + 1) * SB_TK)
        for pair in range(SB_PAIRS):
            lanes = slice(pair * LANES, (pair + 1) * LANES)
            y = jnp.where(first, acc_ref[sub, pair, 0], acc_ref[sub, pair, 1])
            ys_ref[rows, lanes] = (y * _silu(g_ref[0, rows, lanes])).astype(ys_ref.dtype)

    y = res_ref[...] + jnp.dot(ys_ref[...], w_ref[D_RET:, :], preferred_element_type=F32)
    xn = x_ref[0] + gate_ref[0, pl.ds(bi, 1), :] * y
    if final:
        inv = lax.rsqrt(jnp.mean(xn * xn, axis=-1, keepdims=True) + EPS)
        xn = xn * inv * fg_ref[...]
    o_ref[0] = xn.astype(o_ref.dtype)


def _stick_breaking_outproj(sb, sg, tri, y_ret, w_out, x, mod, final_g, layer):
    b, s, d = x.shape
    final = layer == DEPTH - 1
    rows = lambda width: pl.BlockSpec((1, SB_TQ, width), lambda bi, i: (bi, i, 0))
    once = lambda a: pl.BlockSpec(a.shape, lambda bi, i: (0,) * a.ndim, pipeline_mode=pl.Buffered(1))
    resident = lambda col: pl.BlockSpec((1, s, D_SB), lambda bi, i: (bi, 0, col),
                                        pipeline_mode=pl.Buffered(1))
    return pl.pallas_call(
        functools.partial(_sb_kernel, final=final),
        grid=(b, s // SB_TQ),
        in_specs=[
            rows(D_SB), resident(1), resident(2), rows(D_SB), once(tri),
            rows(D_RET),
            pl.BlockSpec((1, d, d), lambda bi, i: (layer, 0, 0), pipeline_mode=pl.Buffered(1)),
            rows(d),
            pl.BlockSpec((1, ADA_ROWS, d), lambda bi, i: (layer, 0, 2),
                         pipeline_mode=pl.Buffered(1)),
            pl.BlockSpec((1, d), lambda bi, i: (0, 0)),
        ],
        out_specs=rows(d),
        out_shape=jax.ShapeDtypeStruct((b, s, d), x.dtype),
        scratch_shapes=[
            pltpu.VMEM((SB_SUB, SB_PAIRS, 2, SB_TK, LANES), BF16),
            pltpu.VMEM((SB_SUB, SB_PAIRS, 2, SB_TK, LANES), F32),
            pltpu.VMEM((SB_SUB, SB_PAIRS, 2, SB_TK, LANES), F32),
            pltpu.VMEM((2, SB_PAIRS, 2, SB_TK, SB_TK), F32),
            pltpu.VMEM((2, SB_PAIRS, 2, SB_TK, SB_TK), BF16),
            pltpu.VMEM((SB_TQ, D_SB), BF16),
            pltpu.VMEM((SB_TQ, D_MODEL), F32),
            pltpu.VMEM((D_MODEL, D_MODEL), BF16),
        ],
        compiler_params=pltpu.CompilerParams(dimension_semantics=("arbitrary", "arbitrary"),
                                             vmem_limit_bytes=56 * MIB),
        name="stick_breaking_out_proj_final" if final else "stick_breaking_out_proj",
    )(sb, sb, sb, sg, tri, y_ret, w_out, x, mod, final_g.reshape(1, d))


def kernel(x, c, norm_g, w_ada, b_ada, w_in, w_out, final_g):
    b, s, d = x.shape
    assert d == D_MODEL and s % max(SB_TQ, PROJ_TM) == 0
    c_pad = jnp.zeros((ADA_ROWS, d), F32).at[:b].set(c.astype(F32))
    mod = _ada(c_pad, w_ada, b_ada)
    tables = _retention_tables(s)
    tri = jnp.asarray(np.tril(np.ones((SB_TK, SB_TK))), dtype=BF16)

    h = x.astype(F32)
    for layer in range(DEPTH):
        y_ret, sb, sg = _inproj(h, norm_g, mod, w_in, tables, layer)
        h = _stick_breaking_outproj(sb, sg, tri, y_ret, w_out, h, mod, final_g, layer)
    return h.astype(x.dtype)
```

```python
import functools

import jax
import jax.numpy as jnp
import numpy as np
from jax import lax
from jax.experimental import pallas as pl
from jax.experimental.pallas import tpu as pltpu

D_MODEL = 1024
DEPTH = 2
CHUNK = 64
D_RET = 512
D_SB = 512
RET_HEADS = 4
RET_HEAD_DIM = 128
SB_HEADS = 8
SB_HEAD_DIM = 64
ROPE_BASE = 10000.0
EPS = 1e-6
LOG2E = 1.4426950408889634
LOG2E_HI = 1.4453125
LOG2E_LO = LOG2E - LOG2E_HI

LANES = 128
SB_PAIRS = SB_HEADS * SB_HEAD_DIM // LANES
MIB = 1024 * 1024

ADA_ROWS = 8
ADA_TN = 1536
PROJ_TN = 512
RET_L = 256
PROJ_TM = 2 * RET_L
SB_TK = 256
SB_SUB = 2
SB_TQ = SB_SUB * SB_TK
SB_EAGER = 192

SB_EXP2_CLAMP = 126.0
SB_DEAD = 151.0
SB_MASKED = -1e30
SB_Q_SCALE = LOG2E * SB_HEAD_DIM ** -0.5

F32 = jnp.float32
BF16 = jnp.bfloat16


def _silu(x):
    return x * (1.0 / (1.0 + jnp.exp(-x)))


def _ada_kernel(c_ref, w_ref, b_ref, o_ref):
    c_act = _silu(c_ref[...])
    o_ref[0] = jnp.dot(c_act.astype(BF16), w_ref[0].astype(BF16),
                       preferred_element_type=F32) + b_ref[0]


def _ada(c_pad, w_ada, b_ada):
    n = 3 * D_MODEL
    return pl.pallas_call(
        _ada_kernel,
        grid=(DEPTH, n // ADA_TN),
        in_specs=[
            pl.BlockSpec((ADA_ROWS, D_MODEL), lambda l, j: (0, 0)),
            pl.BlockSpec((1, D_MODEL, ADA_TN), lambda l, j: (l, 0, j)),
            pl.BlockSpec((1, 1, ADA_TN), lambda l, j: (l, 0, j)),
        ],
        out_specs=pl.BlockSpec((1, ADA_ROWS, ADA_TN), lambda l, j: (l, 0, j)),
        out_shape=jax.ShapeDtypeStruct((DEPTH, ADA_ROWS, n), F32),
        name="ada_mod",
    )(c_pad, w_ada, b_ada.reshape(DEPTH, 1, n))


def _retention_tables(seq):
    half = RET_HEAD_DIM // 2
    pos = np.arange(seq, dtype=np.float64)
    inv = ROPE_BASE ** (-np.arange(half, dtype=np.float64) / half)
    ang = pos[:, None] * inv[None, :]
    cos, sin = np.cos(ang), np.sin(ang)
    cos2 = np.concatenate([cos, cos], axis=-1)
    sin2 = np.concatenate([-sin, sin], axis=-1)

    log_gamma = np.log1p(-(2.0 ** (-5.0 - np.arange(RET_HEADS, dtype=np.float64))))
    idx = np.arange(RET_L, dtype=np.float64)
    diff = idx[:, None] - idx[None, :]
    same = (np.arange(RET_L)[:, None] // CHUNK) == (np.arange(RET_L)[None, :] // CHUNK)
    expo = np.where(same, np.abs(diff), diff)
    dmask = np.where(same | (diff > 0), np.exp(expo[None] * log_gamma[:, None, None]), 0.0)
    q_dec = np.exp((idx + 1.0)[None, :] * log_gamma[:, None])
    k_dec = np.exp((RET_L - 1.0 - idx)[None, :] * log_gamma[:, None])
    blk_dec = np.exp(RET_L * log_gamma)
    rep = lambda t: np.broadcast_to(t[:, :, None], (RET_HEADS, RET_L, LANES))
    blk = np.broadcast_to(blk_dec[:, None, None], (RET_HEADS, 1, LANES))
    return tuple(jnp.asarray(t, dtype=F32) for t in (cos2, sin2, dmask, rep(q_dec), rep(k_dec), blk))


def _retention_head(h, rq, rk, rv, rg, cos, sin, dm_ref, qd_ref, kd_ref, bd_ref, state_ref):
    half = RET_HEAD_DIM // 2
    cols = slice(h * RET_HEAD_DIM, (h + 1) * RET_HEAD_DIM)
    q = rq[:, cols]
    k = rk[:, cols]
    q = q * cos + pltpu.roll(q, half, axis=1) * sin
    k = (k * cos + pltpu.roll(k, half, axis=1) * sin) * (RET_HEAD_DIM ** -0.5)
    vb = rv[:, cols].astype(BF16)
    qb = q.astype(BF16)
    scores = lax.dot_general(qb, k.astype(BF16), (((1,), (1,)), ((), ())),
                             preferred_element_type=F32) * dm_ref[h]
    o = jnp.dot(scores.astype(BF16), vb, preferred_element_type=F32)
    state = state_ref[h]
    o = o + jnp.dot(qb, state.astype(BF16), preferred_element_type=F32) * qd_ref[h]
    kv = lax.dot_general((k * kd_ref[h]).astype(BF16), vb, (((0,), (0,)), ((), ())),
                         preferred_element_type=F32)
    state_ref[h] = state * bd_ref[h] + kv
    mu = jnp.mean(o, axis=-1, keepdims=True)
    oc = o - mu
    var = jnp.mean(oc * oc, axis=-1, keepdims=True)
    return oc * lax.rsqrt(var + EPS) * _silu(rg[:, cols])


def _inproj_kernel(x_ref, g_ref, shift_ref, scale_ref, w32_ref, cos_ref, sin_ref, dm_ref, qd_ref,
                   kd_ref, bd_ref, yret_ref, sb_ref, sg_ref, state_ref, w_ref):
    bi = pl.program_id(0)

    @pl.when(jnp.logical_and(bi == 0, pl.program_id(1) == 0))
    def _():
        w_ref[...] = w32_ref[0].astype(BF16)

    @pl.when(pl.program_id(1) == 0)
    def _():
        state_ref[...] = jnp.zeros_like(state_ref)

    x = x_ref[0]
    inv = lax.rsqrt(jnp.mean(x * x, axis=-1, keepdims=True) + EPS)
    scale = scale_ref[0, pl.ds(bi, 1), :]
    shift = shift_ref[0, pl.ds(bi, 1), :]
    h = (x * inv * g_ref[0]) * (1.0 + scale) + shift
    hb = h.astype(BF16)
    proj = lambda n: jnp.dot(hb, w_ref[:, n * PROJ_TN:(n + 1) * PROJ_TN], preferred_element_type=F32)
    n_ret = 4 * D_RET // PROJ_TN
    n_qkv = 3 * D_SB // PROJ_TN
    ret = [proj(n) for n in range(n_ret)]
    for blk in range(PROJ_TM // RET_L):
        rows = slice(blk * RET_L, (blk + 1) * RET_L)
        rq, rk, rv, rg = [r[rows] for r in ret]
        cos = cos_ref[rows, :]
        sin = sin_ref[rows, :]
        for hd in range(RET_HEADS):
            y = _retention_head(hd, rq, rk, rv, rg, cos, sin, dm_ref, qd_ref, kd_ref, bd_ref,
                                state_ref)
            yret_ref[0, rows, hd * RET_HEAD_DIM:(hd + 1) * RET_HEAD_DIM] = y.astype(yret_ref.dtype)
    for n in range(n_qkv):
        p = proj(n_ret + n)
        if n * PROJ_TN < D_SB:
            p = p * SB_Q_SCALE
        sb_ref[0, :, n * PROJ_TN:(n + 1) * PROJ_TN] = p.astype(BF16)
    sg_ref[0] = proj(n_ret + n_qkv)


def _inproj(x, norm_g, mod, w_in, tables, layer):
    b, s, d = x.shape
    d_in = w_in.shape[2]
    cos2, sin2, dmask, q_dec, k_dec, blk = tables
    once = lambda a: pl.BlockSpec(a.shape, lambda bi, i: (0,) * a.ndim, pipeline_mode=pl.Buffered(1))
    of_layer = lambda shape, col=0: pl.BlockSpec((1,) + shape, lambda bi, i: (layer, 0, col),
                                                 pipeline_mode=pl.Buffered(1))
    return pl.pallas_call(
        _inproj_kernel,
        grid=(b, s // PROJ_TM),
        in_specs=[
            pl.BlockSpec((1, PROJ_TM, d), lambda bi, i: (bi, i, 0)),
            of_layer((1, d)),
            of_layer((ADA_ROWS, d), 0), of_layer((ADA_ROWS, d), 1),
            of_layer((d, d_in)),
            pl.BlockSpec((PROJ_TM, LANES), lambda bi, i: (i, 0)),
            pl.BlockSpec((PROJ_TM, LANES), lambda bi, i: (i, 0)),
            once(dmask), once(q_dec), once(k_dec), once(blk),
        ],
        out_specs=[
            pl.BlockSpec((1, PROJ_TM, D_RET), lambda bi, i: (bi, i, 0)),
            pl.BlockSpec((1, PROJ_TM, 3 * D_SB), lambda bi, i: (bi, i, 0)),
            pl.BlockSpec((1, PROJ_TM, D_SB), lambda bi, i: (bi, i, 0)),
        ],
        out_shape=[
            jax.ShapeDtypeStruct((b, s, D_RET), BF16),
            jax.ShapeDtypeStruct((b, s, 3 * D_SB), BF16),
            jax.ShapeDtypeStruct((b, s, D_SB), F32),
        ],
        scratch_shapes=[pltpu.VMEM((RET_HEADS, RET_HEAD_DIM, RET_HEAD_DIM), F32),
                        pltpu.VMEM((d, d_in), BF16)],
        compiler_params=pltpu.CompilerParams(dimension_semantics=("arbitrary", "arbitrary"),
                                             vmem_limit_bytes=56 * MIB),
        name="in_proj_retention",
    )(x, norm_g.reshape(DEPTH, 1, d), mod, mod, w_in, cos2, sin2, dmask, q_dec, k_dec, blk)


def _softplus2(u):
    ln = jnp.log(1.0 + jnp.exp2(jnp.minimum(u, SB_EXP2_CLAMP)))
    return jnp.maximum(u, ln * LOG2E_HI + ln * LOG2E_LO)


def _sb_scores(qm_ref, u_ref, sp_ref, sub, slot, k_blk, diagonal, bias=None,
               rows=(0, SB_TK), nk=SB_TK):
    r0, r1 = rows
    n = r1 - r0
    assert not diagonal or nk >= r1
    for pair in range(SB_PAIRS):
        kb = k_blk[:nk, pair * LANES:(pair + 1) * LANES]
        qm = qm_ref[sub, pair, :, r0:r1, :].reshape(2 * n, LANES)
        u = lax.dot_general(qm, kb, (((1,), (1,)), ((), ())), preferred_element_type=F32)
        if bias is not None:
            u = u + bias
        sp = _softplus2(u.astype(BF16))
        if diagonal:
            t_loc = lax.broadcasted_iota(jnp.int32, (n, nk), 0) + r0
            s_loc = lax.broadcasted_iota(jnp.int32, (n, nk), 1)
            m = s_loc < t_loc
            mask = jnp.concatenate([m, m], axis=0)
            sp = jnp.where(mask, sp, jnp.zeros_like(sp))
            u = jnp.where(mask, u, SB_MASKED)
        u_ref[slot, pair, :, r0:r1, :nk] = u.reshape(2, n, nk)
        sp_ref[slot, pair, :, r0:r1, :nk] = sp.reshape(2, n, nk)


def _sb_values(tri_ref, u_ref, sp_ref, acc_ref, ncarry_ref, sub, slot, v_blk,
               rows=(0, SB_TK), nk=SB_TK):
    r0, r1 = rows
    n = r1 - r0
    for pair in range(SB_PAIRS):
        vb = v_blk[:nk, pair * LANES:(pair + 1) * LANES]
        sp = sp_ref[slot, pair, :, r0:r1, :nk].reshape(2 * n, nk)
        r = jnp.dot(sp, tri_ref[:nk, :nk], preferred_element_type=F32)
        u = u_ref[slot, pair, :, r0:r1, :nk].reshape(2 * n, nk)
        ncarry = ncarry_ref[sub, pair, :, r0:r1, :].reshape(2 * n, LANES)
        a = jnp.exp2(u - r + jnp.concatenate([ncarry] * (nk // LANES), axis=1))
        pv = jnp.dot(a.astype(BF16), vb, preferred_element_type=F32)
        acc_ref[sub, pair, :, r0:r1, :] += pv.reshape(2, n, LANES)
        ncarry_ref[sub, pair, :, r0:r1, :] = (ncarry - r[:, :1]).reshape(2, n, LANES)


def _sb_kernel(q_ref, kw0_ref, kw1_ref, kw2_ref, vw0_ref, vw1_ref, vw2_ref, g_ref, tri_ref, yr_ref,
               w32_ref, x_ref, gate_ref, fg_ref, kv_hbm, o_ref,
               qm_ref, acc_ref, ncarry_ref, u_ref, sp_ref, ys_ref, res_ref, w_ref,
               kf_ref, vf_ref, sem, *, final):
    step = pl.program_id(1)
    bi = pl.program_id(0)

    @pl.when(jnp.logical_and(bi == 0, step == 0))
    def _():
        w_ref[...] = w32_ref[0].astype(BF16)

    res_ref[...] = jnp.dot(yr_ref[0], w_ref[:D_RET, :], preferred_element_type=F32)
    lane = lax.broadcasted_iota(jnp.int32, (SB_TK, LANES), 1)
    first = lane < SB_HEAD_DIM
    for sub in range(SB_SUB):
        for pair in range(SB_PAIRS):
            q = q_ref[0, sub * SB_TK:(sub + 1) * SB_TK, pair * LANES:(pair + 1) * LANES]
            zero = jnp.zeros_like(q)
            qm_ref[sub, pair, 0] = jnp.where(first, q, zero)
            qm_ref[sub, pair, 1] = jnp.where(first, zero, q)
    acc_ref[...] = jnp.zeros_like(acc_ref)
    ncarry_ref[...] = jnp.zeros_like(ncarry_ref)

    scores = functools.partial(_sb_scores, qm_ref, u_ref, sp_ref)
    values = functools.partial(_sb_values, tri_ref, u_ref, sp_ref, acc_ref, ncarry_ref)
    k_win = [r.at[0] for r in (kw0_ref, kw1_ref, kw2_ref)]
    v_win = [r.at[0] for r in (vw0_ref, vw1_ref, vw2_ref)]

    own = lambda sub: step * SB_SUB + sub
    prev_bias = lambda sub: jnp.where(step == 0, SB_MASKED, 0.0) if sub == 0 else None
    eager, late = (0, SB_EAGER), (SB_EAGER, SB_TK)
    sequence = []
    half = SB_TK // 2
    for sub in range(SB_SUB):
        sequence.append((sub, 1 + sub, True, None, (0, half), half))
        sequence.append((sub, 1 + sub, True, None, (half, SB_TK), SB_TK))
        sequence.append((sub, sub, False, prev_bias(sub), eager, SB_TK))
    for n, (sub, win, diagonal, bias, rows, nk) in enumerate(sequence):
        scores(sub, n % 2, k_win[win], diagonal, bias, rows, nk)
        if n > 0:
            p_sub, p_win, _, _, p_rows, p_nk = sequence[n - 1]
            values(p_sub, (n - 1) % 2, v_win[p_win], p_rows, p_nk)
    p_sub, p_win, _, _, p_rows, p_nk = sequence[-1]
    values(p_sub, (len(sequence) - 1) % 2, v_win[p_win], p_rows, p_nk)

    def fetch(blk):
        rows = pl.ds(pl.multiple_of(blk * SB_TK, SB_TK), SB_TK)
        copies = [pltpu.make_async_copy(kv_hbm.at[bi, rows, pl.ds((1 + j) * D_SB, D_SB)], dst,
                                        sem.at[j]) for j, dst in enumerate((kf_ref, vf_ref))]
        for c in copies:
            c.start()
        for c in copies:
            c.wait()

    live = lambda sub, r0=0: (jnp.max(ncarry_ref[sub, :, :, r0:, :]) > -SB_DEAD).astype(jnp.int32)
    live_late = [live(sub, SB_EAGER) for sub in range(SB_SUB)]
    live_now = [live(sub) for sub in range(SB_SUB)]
    for sub in range(SB_SUB):
        @pl.when(live_late[sub] > 0)
        def _():
            scores(sub, 0, k_win[sub], False, prev_bias(sub), late)
            values(sub, 0, v_win[sub], late)

        def body(state):
            blk, _ = state
            fetch(blk)
            scores(sub, 0, kf_ref, False)
            values(sub, 0, vf_ref)
            return blk - 1, live(sub)

        lax.while_loop(lambda state: jnp.logical_and(state[0] >= 0, state[1] > 0),
                       body, (own(sub) - 2, live_now[sub]))

    for sub in range(SB_SUB):
        rows = slice(sub * SB_TK, (sub + 1) * SB_TK)
        for pair in range(SB_PAIRS):
            lanes = slice(pair * LANES, (pair + 1) * LANES)
            y = jnp.where(first, acc_ref[sub, pair, 0], acc_ref[sub, pair, 1])
            ys_ref[rows, lanes] = (y * _silu(g_ref[0, rows, lanes])).astype(ys_ref.dtype)

    y = res_ref[...] + jnp.dot(ys_ref[...], w_ref[D_RET:, :], preferred_element_type=F32)
    xn = x_ref[0] + gate_ref[0, pl.ds(bi, 1), :] * y
    if final:
        inv = lax.rsqrt(jnp.mean(xn * xn, axis=-1, keepdims=True) + EPS)
        xn = xn * inv * fg_ref[...]
    o_ref[0] = xn.astype(o_ref.dtype)


def _stick_breaking_outproj(sb, sg, tri, y_ret, w_out, x, mod, final_g, layer):
    b, s, d = x.shape
    final = layer == DEPTH - 1
    rows = lambda width: pl.BlockSpec((1, SB_TQ, width), lambda bi, i: (bi, i, 0))
    once = lambda a: pl.BlockSpec(a.shape, lambda bi, i: (0,) * a.ndim, pipeline_mode=pl.Buffered(1))
    window = lambda col, w: pl.BlockSpec(
        (1, SB_TK, D_SB), lambda bi, i: (bi, jnp.maximum(SB_SUB * i - 1 + w, 0), col))
    return pl.pallas_call(
        functools.partial(_sb_kernel, final=final),
        grid=(b, s // SB_TQ),
        in_specs=[
            rows(D_SB), *[window(1, w) for w in range(3)], *[window(2, w) for w in range(3)],
            rows(D_SB), once(tri),
            rows(D_RET),
            pl.BlockSpec((1, d, d), lambda bi, i: (layer, 0, 0), pipeline_mode=pl.Buffered(1)),
            rows(d),
            pl.BlockSpec((1, ADA_ROWS, d), lambda bi, i: (layer, 0, 2),
                         pipeline_mode=pl.Buffered(1)),
            pl.BlockSpec((1, d), lambda bi, i: (0, 0)),
            pl.BlockSpec(memory_space=pl.ANY),
        ],
        out_specs=rows(d),
        out_shape=jax.ShapeDtypeStruct((b, s, d), x.dtype),
        scratch_shapes=[
            pltpu.VMEM((SB_SUB, SB_PAIRS, 2, SB_TK, LANES), BF16),
            pltpu.VMEM((SB_SUB, SB_PAIRS, 2, SB_TK, LANES), F32),
            pltpu.VMEM((SB_SUB, SB_PAIRS, 2, SB_TK, LANES), F32),
            pltpu.VMEM((2, SB_PAIRS, 2, SB_TK, SB_TK), F32),
            pltpu.VMEM((2, SB_PAIRS, 2, SB_TK, SB_TK), BF16),
            pltpu.VMEM((SB_TQ, D_SB), BF16),
            pltpu.VMEM((SB_TQ, D_MODEL), F32),
            pltpu.VMEM((D_MODEL, D_MODEL), BF16),
            pltpu.VMEM((SB_TK, D_SB), BF16),
            pltpu.VMEM((SB_TK, D_SB), BF16),
            pltpu.SemaphoreType.DMA((2,)),
        ],
        compiler_params=pltpu.CompilerParams(dimension_semantics=("arbitrary", "arbitrary"),
                                             vmem_limit_bytes=56 * MIB),
        name="stick_breaking_out_proj_final" if final else "stick_breaking_out_proj",
    )(sb, sb, sb, sb, sb, sb, sb, sg, tri, y_ret, w_out, x, mod, final_g.reshape(1, d), sb)


def kernel(x, c, norm_g, w_ada, b_ada, w_in, w_out, final_g):
    b, s, d = x.shape
    assert d == D_MODEL and s % max(SB_TQ, PROJ_TM) == 0 and SB_SUB == 2
    c_pad = jnp.zeros((ADA_ROWS, d), F32).at[:b].set(c.astype(F32))
    mod = _ada(c_pad, w_ada, b_ada)
    tables = _retention_tables(s)
    tri = jnp.asarray(np.tril(np.ones((SB_TK, SB_TK))), dtype=BF16)

    h = x.astype(F32)
    for layer in range(DEPTH):
        y_ret, sb, sg = _inproj(h, norm_g, mod, w_in, tables, layer)
        h = _stick_breaking_outproj(sb, sg, tri, y_ret, w_out, h, mod, final_g, layer)
    return h.astype(x.dtype)
```
